```python
import jax, jax.numpy as jnp
from jax import lax
import numpy as np

D_MODEL = 2048
BATCH = 4
SEQ = 4096
DEPTH = 1

D_RNN = 2560
RNN_BLOCKS = 16
RNN_BLOCK_DIM = D_RNN // RNN_BLOCKS
CONV_WIDTH = 4
LRU_C = 8.0
N_HEADS = 16
HEAD_DIM = 128
N_KV = 4
HEADS_PER_KV = N_HEADS // N_KV
CMP_BLOCK = 32
CMP_STRIDE = 16
CMP_HIDDEN = 256
SEL_BLOCK = 64
SEL_TOPK = 16
WINDOW = 512
Q_BLOCK = 128
N_EXPERTS = 32
TOP_K = 4
D_FF = 2048
SWIGLU_LIMIT = 7.0
SWIGLU_ALPHA = 1.702
MOE_BLOCK = 128
PLE_DIM = 256

EPS = 1e-6
NEG = -1e30
FORCE = 1e9
N_IN = 2 * D_RNN + N_HEADS * HEAD_DIM + 6 * N_KV * HEAD_DIM + 3 * N_HEADS + 2 * D_MODEL

kernel_name = 'hybrid_rglru_nsa_moe_ple'


def rms_norm(x, g):
    xf = x.astype(jnp.float32)
    y = xf * lax.rsqrt(jnp.mean(xf * xf, axis=-1, keepdims=True) + EPS)
    return (y * g.astype(jnp.float32)).astype(x.dtype)


def alibi_slopes():
    s = 2.0 ** (-8.0 * np.arange(1, N_HEADS + 1) / N_HEADS)
    return jnp.asarray(s, dtype=jnp.float32).reshape(N_KV, HEADS_PER_KV)


def rglru_branch(xr, gr, conv_w, conv_b, w_rg_a, b_rg_a, w_rg_i, b_rg_i, lru_lambda):
    B, S, _ = xr.shape
    xp = jnp.pad(xr, ((0, 0), (CONV_WIDTH - 1, 0), (0, 0)))
    xc = conv_b + sum(conv_w[k] * xp[:, k:k + S] for k in range(CONV_WIDTH))
    xb = xc.reshape(B, S, RNN_BLOCKS, RNN_BLOCK_DIM)
    rec_gate = jax.nn.sigmoid(jnp.einsum('bshi,hij->bshj', xb, w_rg_a) + b_rg_a).reshape(B, S, D_RNN)
    in_gate = jax.nn.sigmoid(jnp.einsum('bshi,hij->bshj', xb, w_rg_i) + b_rg_i).reshape(B, S, D_RNN)
    log_a = -LRU_C * rec_gate.astype(jnp.float32) * jax.nn.softplus(-lru_lambda.astype(jnp.float32))
    a = jnp.exp(log_a)
    u = jnp.sqrt(-jnp.expm1(2.0 * log_a)) * (in_gate * xc).astype(jnp.float32)

    def combine(left, right):
        a1, b1 = left
        a2, b2 = right
        return a1 * a2, a2 * b1 + b2

    _, h = lax.associative_scan(combine, (a, u), axis=1)
    return h.astype(xr.dtype) * jax.nn.gelu(gr)


def compress(xk, pos, w1, w2):
    B, S = xk.shape[:2]
    n_cmp = (S - CMP_BLOCK) // CMP_STRIDE + 1
    cidx = np.arange(n_cmp)[:, None] * CMP_STRIDE + np.arange(CMP_BLOCK)[None, :]
    blk = xk[:, cidx] + pos[None, None, :, None, :]
    blk = jnp.moveaxis(blk, 3, 2).reshape(B, n_cmp, N_KV, CMP_BLOCK * HEAD_DIM)
    return jax.nn.gelu(blk @ w1) @ w2


def nsa_branch(q, kv, gate_logits, q_gain, k_gains, cmp_pos_k, cmp_w1_k, cmp_w2_k, cmp_pos_v, cmp_w1_v, cmp_w2_v):
    B, S, _ = q.shape
    f32 = jnp.float32
    scale = HEAD_DIM ** -0.5
    slopes = alibi_slopes()
    q = rms_norm(q.reshape(B, S, N_HEADS, HEAD_DIM), q_gain).reshape(B, S, N_KV, HEADS_PER_KV, HEAD_DIM)
    kv = kv.reshape(B, S, 6, N_KV, HEAD_DIM)
    kc = rms_norm(compress(kv[:, :, 0], cmp_pos_k, cmp_w1_k, cmp_w2_k), k_gains[0])
    vc = compress(kv[:, :, 1], cmp_pos_v, cmp_w1_v, cmp_w2_v)
    ks = rms_norm(kv[:, :, 2], k_gains[1])
    vs = kv[:, :, 3]
    kw = rms_norm(kv[:, :, 4], k_gains[2])
    vw = kv[:, :, 5]
    t = jnp.arange(S)
    n_cmp = kc.shape[1]

    c_end = jnp.arange(n_cmp) * CMP_STRIDE + CMP_BLOCK - 1
    dist_c = (t[:, None] - c_end[None, :]).astype(f32)
    valid_c = dist_c >= 0
    s_c = jnp.einsum('bsgqd,bcgd->bgqsc', q, kc).astype(f32) * scale - slopes[None, :, :, None, None] * dist_c
    p_c = jax.nn.softmax(jnp.where(valid_c, s_c, NEG), axis=-1) * valid_c
    o_c = jnp.einsum('bgqsc,bcgd->bsgqd', p_c.astype(vc.dtype), vc)

    n_sel = S // SEL_BLOCK
    c0 = np.arange(n_cmp)[:, None] * CMP_STRIDE
    s0 = np.arange(n_sel)[None, :] * SEL_BLOCK
    overlap = np.clip(np.minimum(c0 + CMP_BLOCK, s0 + SEL_BLOCK) - np.maximum(c0, s0), 0, None) / CMP_STRIDE
    imp = jnp.einsum('bgqsc,cj->bgsj', p_c, jnp.asarray(overlap, f32))
    j = jnp.arange(n_sel)[None, :]
    cur = t[:, None] // SEL_BLOCK
    forced = (j == 0) | (j == cur) | (j == cur - 1)
    future = j * SEL_BLOCK > t[:, None]
    imp = jnp.where(forced, FORCE, jnp.where(future, NEG, imp))
    n_top = min(SEL_TOPK, n_sel)
    _, sel_idx = lax.top_k(imp, n_top)

    ks_blocks = ks.reshape(B, n_sel, SEL_BLOCK, N_KV, HEAD_DIM).transpose(0, 3, 1, 2, 4)
    vs_blocks = vs.reshape(B, n_sel, SEL_BLOCK, N_KV, HEAD_DIM).transpose(0, 3, 1, 2, 4)
    kw_pad = jnp.pad(kw, ((0, 0), (WINDOW, 0), (0, 0), (0, 0)))
    vw_pad = jnp.pad(vw, ((0, 0), (WINDOW, 0), (0, 0), (0, 0)))
    n_qb = S // Q_BLOCK
    q_blocks = q.reshape(B, n_qb, Q_BLOCK, N_KV, HEADS_PER_KV, HEAD_DIM).transpose(1, 0, 2, 3, 4, 5)
    idx_blocks = sel_idx.reshape(B, N_KV, n_qb, Q_BLOCK, n_top).transpose(2, 0, 1, 3, 4)
    bi = jnp.arange(B)[:, None, None, None]
    gi = jnp.arange(N_KV)[None, :, None, None]

    def block_attend(args):
        qb, qblk, idx = args
        tq = qb * Q_BLOCK + jnp.arange(Q_BLOCK)
        k_sel = ks_blocks[bi, gi, idx]
        v_sel = vs_blocks[bi, gi, idx]
        s_pos = idx[..., None] * SEL_BLOCK + jnp.arange(SEL_BLOCK)
        dist = (tq[None, None, :, None, None] - s_pos).astype(f32)[:, :, None]
        s = jnp.einsum('btgqd,bgtnld->bgqtnl', qblk, k_sel).astype(f32) * scale - slopes[None, :, :, None, None, None] * dist
        s = jnp.where(dist >= 0, s, NEG)
        p = jax.nn.softmax(s.reshape(B, N_KV, HEADS_PER_KV, Q_BLOCK, -1), axis=-1).reshape(s.shape)
        o_sel = jnp.einsum('bgqtnl,bgtnld->btgqd', p.astype(v_sel.dtype), v_sel)
        k_win = lax.dynamic_slice_in_dim(kw_pad, qb * Q_BLOCK, Q_BLOCK + WINDOW, axis=1)
        v_win = lax.dynamic_slice_in_dim(vw_pad, qb * Q_BLOCK, Q_BLOCK + WINDOW, axis=1)
        s_pos_w = qb * Q_BLOCK - WINDOW + jnp.arange(Q_BLOCK + WINDOW)
        dist_w = (tq[:, None] - s_pos_w[None, :]).astype(f32)
        valid_w = (dist_w >= 0) & (dist_w < WINDOW) & (s_pos_w[None, :] >= 0)
        sw = jnp.einsum('btgqd,bkgd->bgqtk', qblk, k_win).astype(f32) * scale - slopes[None, :, :, None, None] * dist_w
        pw = jax.nn.softmax(jnp.where(valid_w, sw, NEG), axis=-1)
        o_win = jnp.einsum('bgqtk,bkgd->btgqd', pw.astype(v_win.dtype), v_win)
        return o_sel, o_win

    o_sel, o_win = lax.map(block_attend, (jnp.arange(n_qb), q_blocks, idx_blocks))
    o_sel = o_sel.transpose(1, 0, 2, 3, 4, 5).reshape(B, S, N_KV, HEADS_PER_KV, HEAD_DIM)
    o_win = o_win.transpose(1, 0, 2, 3, 4, 5).reshape(B, S, N_KV, HEADS_PER_KV, HEAD_DIM)

    g = jax.nn.sigmoid(gate_logits.astype(f32)).reshape(B, S, 3, N_KV, HEADS_PER_KV, 1).astype(q.dtype)
    o = g[:, :, 0] * o_c + g[:, :, 1] * o_sel + g[:, :, 2] * o_win
    return o.reshape(B, S, N_HEADS * HEAD_DIM)


def moe(h, w_router, b_router, w_up, b_up, w_down, b_down):
    B, S, D = h.shape
    T = B * S
    xf = h.reshape(T, D)
    logits = (xf @ w_router + b_router).astype(jnp.float32)
    top_vals, top_idx = lax.top_k(logits, TOP_K)
    gates = jax.nn.softmax(top_vals, axis=-1)
    tk = T * TOP_K
    flat_e = top_idx.reshape(tk)
    flat_tok = jnp.arange(tk, dtype=jnp.int32) // TOP_K
    flat_w = gates.reshape(tk)
    order = jnp.argsort(flat_e)
    se, stok, sw = flat_e[order], flat_tok[order], flat_w[order]
    counts = jnp.bincount(flat_e, length=N_EXPERTS)
    padded = (counts + MOE_BLOCK - 1) // MOE_BLOCK * MOE_BLOCK
    start = jnp.cumsum(counts) - counts
    pend = jnp.cumsum(padded)
    pstart = pend - padded
    dest = pstart[se] + jnp.arange(tk) - start[se]
    n_blocks = -(-tk // MOE_BLOCK) + N_EXPERTS
    buf_tok = jnp.full((n_blocks * MOE_BLOCK,), T, jnp.int32).at[dest].set(stok)
    buf_w = jnp.zeros((n_blocks * MOE_BLOCK,), gates.dtype).at[dest].set(sw)
    block_e = jnp.minimum(jnp.sum(jnp.arange(n_blocks)[:, None] * MOE_BLOCK >= pend[None, :], axis=1), N_EXPERTS - 1)
    x_pad = jnp.concatenate([xf, jnp.zeros((1, D), xf.dtype)], axis=0)
    xb = x_pad[buf_tok].reshape(n_blocks, MOE_BLOCK, D)

    def expert_block(args):
        xblk, e = args
        hu = xblk @ w_up[e] + b_up[e]
        h_glu = jnp.minimum(hu[:, :D_FF], SWIGLU_LIMIT)
        h_lin = jnp.clip(hu[:, D_FF:], -SWIGLU_LIMIT, SWIGLU_LIMIT)
        return (h_glu * jax.nn.sigmoid(SWIGLU_ALPHA * h_glu) * (h_lin + 1.0)) @ w_down[e] + b_down[e]

    yb = lax.map(expert_block, (xb, block_e)).reshape(n_blocks * MOE_BLOCK, D)
    y = jax.ops.segment_sum(yb * buf_w[:, None].astype(yb.dtype), buf_tok, num_segments=T + 1)[:T]
    return y.reshape(B, S, D).astype(h.dtype)


def setup_inputs(seed: int = 0) -> dict:
    key = jax.random.key(seed)
    keys = iter(jax.random.split(key, 40))
    L = DEPTH
    f32 = jnp.float32

    def nrm(shape, scale):
        return jax.random.normal(next(keys), shape, f32) * scale

    def gain(shape):
        return 1.0 + nrm(shape, 0.02)

    x = nrm((BATCH, SEQ, D_MODEL), 1.0)
    p = nrm((DEPTH, BATCH, SEQ, PLE_DIM), 1.0)
    norm_mix = gain((L, D_MODEL))
    w_in = nrm((L, D_MODEL, N_IN), D_MODEL ** -0.5)
    conv_w = nrm((L, CONV_WIDTH, D_RNN), CONV_WIDTH ** -0.5)
    conv_b = nrm((L, D_RNN), 0.02)
    w_rg_a = nrm((L, RNN_BLOCKS, RNN_BLOCK_DIM, RNN_BLOCK_DIM), RNN_BLOCK_DIM ** -0.5)
    b_rg_a = nrm((L, RNN_BLOCKS, RNN_BLOCK_DIM), 0.02)
    w_rg_i = nrm((L, RNN_BLOCKS, RNN_BLOCK_DIM, RNN_BLOCK_DIM), RNN_BLOCK_DIM ** -0.5)
    b_rg_i = nrm((L, RNN_BLOCKS, RNN_BLOCK_DIM), 0.02)
    a8 = jax.random.uniform(next(keys), (L, D_RNN), f32, minval=0.9, maxval=0.999)
    a_base = a8 ** (1.0 / LRU_C)
    lru_lambda = jnp.log(a_base) - jnp.log1p(-a_base)
    w_rnn_out = nrm((L, D_RNN, D_MODEL), D_RNN ** -0.5)
    q_gain = gain((L, HEAD_DIM))
    k_gains = gain((L, 3, HEAD_DIM))
    cmp_pos_k = nrm((L, CMP_BLOCK, HEAD_DIM), 0.02)
    cmp_w1_k = nrm((L, CMP_BLOCK * HEAD_DIM, CMP_HIDDEN), (CMP_BLOCK * HEAD_DIM) ** -0.5)
    cmp_w2_k = nrm((L, CMP_HIDDEN, HEAD_DIM), CMP_HIDDEN ** -0.5)
    cmp_pos_v = nrm((L, CMP_BLOCK, HEAD_DIM), 0.02)
    cmp_w1_v = nrm((L, CMP_BLOCK * HEAD_DIM, CMP_HIDDEN), (CMP_BLOCK * HEAD_DIM) ** -0.5)
    cmp_w2_v = nrm((L, CMP_HIDDEN, HEAD_DIM), CMP_HIDDEN ** -0.5)
    w_nsa_out = nrm((L, N_HEADS * HEAD_DIM, D_MODEL), (N_HEADS * HEAD_DIM) ** -0.5)
    w_out = nrm((L, D_MODEL, D_MODEL), D_MODEL ** -0.5)
    norm_moe = gain((L, D_MODEL))
    w_router = nrm((L, D_MODEL, N_EXPERTS), D_MODEL ** -0.5)
    b_router = nrm((L, N_EXPERTS), 0.01)
    w_up = nrm((L, N_EXPERTS, D_MODEL, 2 * D_FF), D_MODEL ** -0.5)
    b_up = nrm((L, N_EXPERTS, 2 * D_FF), 0.02)
    w_down = nrm((L, N_EXPERTS, D_FF, D_MODEL), D_FF ** -0.5)
    b_down = nrm((L, N_EXPERTS, D_MODEL), 0.02)
    w_ple_proj = nrm((L, PLE_DIM, D_MODEL), PLE_DIM ** -0.5)
    ple_norm = gain((L, D_MODEL))
    w_ple_gate = nrm((L, D_MODEL, D_MODEL), D_MODEL ** -0.5)
    return {'x': x, 'p': p, 'norm_mix': norm_mix, 'w_in': w_in, 'conv_w': conv_w, 'conv_b': conv_b,
            'w_rg_a': w_rg_a, 'b_rg_a': b_rg_a, 'w_rg_i': w_rg_i, 'b_rg_i': b_rg_i, 'lru_lambda': lru_lambda,
            'w_rnn_out': w_rnn_out, 'q_gain': q_gain, 'k_gains': k_gains,
            'cmp_pos_k': cmp_pos_k, 'cmp_w1_k': cmp_w1_k, 'cmp_w2_k': cmp_w2_k,
            'cmp_pos_v': cmp_pos_v, 'cmp_w1_v': cmp_w1_v, 'cmp_w2_v': cmp_w2_v,
            'w_nsa_out': w_nsa_out, 'w_out': w_out, 'norm_moe': norm_moe,
            'w_router': w_router, 'b_router': b_router, 'w_up': w_up, 'b_up': b_up,
            'w_down': w_down, 'b_down': b_down,
            'w_ple_proj': w_ple_proj, 'ple_norm': ple_norm, 'w_ple_gate': w_ple_gate}


def reference(x, p, norm_mix, w_in, conv_w, conv_b, w_rg_a, b_rg_a, w_rg_i, b_rg_i, lru_lambda, w_rnn_out,
              q_gain, k_gains, cmp_pos_k, cmp_w1_k, cmp_w2_k, cmp_pos_v, cmp_w1_v, cmp_w2_v, w_nsa_out, w_out,
              norm_moe, w_router, b_router, w_up, b_up, w_down, b_down, w_ple_proj, ple_norm, w_ple_gate):
    B, S, D = x.shape
    q_w = N_HEADS * HEAD_DIM
    kv_w = 6 * N_KV * HEAD_DIM
    splits = [D_RNN, 2 * D_RNN, 2 * D_RNN + q_w, 2 * D_RNN + q_w + kv_w, N_IN - 2 * D_MODEL]
    for i in range(DEPTH):
        h = rms_norm(x, norm_mix[i])
        proj = h @ w_in[i]
        xr, gr, q, kv, g_nsa, g_merge = jnp.split(proj, splits, axis=-1)
        y_rnn = rglru_branch(xr, gr, conv_w[i], conv_b[i], w_rg_a[i], b_rg_a[i], w_rg_i[i], b_rg_i[i],
                             lru_lambda[i]) @ w_rnn_out[i]
        y_nsa = nsa_branch(q, kv, g_nsa, q_gain[i], k_gains[i], cmp_pos_k[i], cmp_w1_k[i], cmp_w2_k[i],
                           cmp_pos_v[i], cmp_w1_v[i], cmp_w2_v[i]) @ w_nsa_out[i]
        m = jax.nn.sigmoid(g_merge).reshape(B, S, 2, D)
        x = x + (m[:, :, 0] * y_rnn + m[:, :, 1] * y_nsa) @ w_out[i]
        x = x + moe(rms_norm(x, norm_moe[i]), w_router[i], b_router[i], w_up[i], b_up[i], w_down[i], b_down[i])
        ple = rms_norm(p[i] @ w_ple_proj[i], ple_norm[i])
        x = x + jax.nn.sigmoid(x @ w_ple_gate[i]) * ple
    return x
```

```python
import functools
import math

import numpy as np
import jax
import jax.numpy as jnp
from jax import lax
from jax.experimental import pallas as pl
from jax.experimental.pallas import tpu as pltpu

RNN_BLOCKS = 16
CONV_WIDTH = 4
LRU_C = 8.0
N_HEADS = 16
HEAD_DIM = 128
N_KV = 4
HEADS_PER_KV = N_HEADS // N_KV
CMP_BLOCK = 32
CMP_STRIDE = 16
SEL_BLOCK = 64
SEL_TOPK = 16
WINDOW = 512
N_EXPERTS = 32
TOP_K = 4
SWIGLU_LIMIT = 7.0
SWIGLU_ALPHA = 1.702
EPS = 1e-6
NEG = -1e30
FORCE = 1e9

LANES = 128
SUBLANES = 8
RNN_SUPER = 640
VMEM_LIMIT = 56 * 1024 * 1024
MOE_ROWS = 256

bf16 = jnp.bfloat16
f32 = jnp.float32


def _cparams(sem, **kw):
    return pltpu.CompilerParams(dimension_semantics=sem, vmem_limit_bytes=VMEM_LIMIT, **kw)


def _gelu(x):
    return 0.5 * x * (1.0 + jnp.tanh(math.sqrt(2.0 / math.pi) * (x + 0.044715 * (x * x * x))))


def _sigmoid(x):
    return 1.0 / (1.0 + jnp.exp(-x))


def _rms(x, gain):
    return x * lax.rsqrt(jnp.mean(x * x, axis=-1, keepdims=True) + EPS) * gain


def _mm_body(*refs, has_gain, n_extra, epilogue, stage):
    i = 0
    a_ref = refs[i]; i += 1
    gain_ref = None
    if has_gain:
        gain_ref = refs[i]; i += 1
    w_ref = refs[i]; i += 1
    extra = refs[i:i + n_extra]; i += n_extra
    o_ref = refs[i]; i += 1
    if stage:
        h_ref = refs[i]

        @pl.when(pl.program_id(1) == 0)
        def _():
            a = a_ref[...].astype(f32)
            if has_gain:
                a = _rms(a, gain_ref[...])
            h_ref[...] = a.astype(bf16)

        lhs = h_ref[...]
    else:
        lhs = a_ref[...]
    acc = jnp.dot(lhs, w_ref[...], preferred_element_type=f32)
    o_ref[...] = epilogue(acc, *[e[...] for e in extra]).astype(o_ref.dtype)


def _matmul(a, w, *, tm, tn, out_dtype, gain=None, extras=(), epilogue=None, name="mm"):
    M, K = a.shape
    N = w.shape[1]
    assert M % tm == 0 and N % tn == 0
    stage = gain is not None or a.dtype != bf16
    if epilogue is None:
        epilogue = lambda acc: acc
    in_specs = [pl.BlockSpec((tm, K), lambda i, j: (i, 0))]
    args = [a]
    if gain is not None:
        in_specs.append(pl.BlockSpec((1, K), lambda i, j: (0, 0)))
        args.append(gain.reshape(1, K).astype(f32))
    in_specs.append(pl.BlockSpec((K, tn), lambda i, j: (0, j)))
    args.append(w)
    for arr, col0 in extras:
        assert col0 % tn == 0
        off = col0 // tn
        in_specs.append(pl.BlockSpec((tm, tn), lambda i, j, off=off: (i, j + off)))
        args.append(arr)
    body = functools.partial(_mm_body, has_gain=gain is not None, n_extra=len(extras), epilogue=epilogue,
                             stage=stage)
    return pl.pallas_call(
        body,
        grid=(M // tm, N // tn),
        in_specs=in_specs,
        out_specs=pl.BlockSpec((tm, tn), lambda i, j: (i, j)),
        out_shape=jax.ShapeDtypeStruct((M, N), out_dtype),
        scratch_shapes=[pltpu.VMEM((tm, K), bf16)] if stage else [],
        compiler_params=_cparams(("parallel", "arbitrary")),
        name=name,
    )(*args)


def _rglru_body(xr_ref, gr_ref, cw_ref, cb_ref, wa_ref, ba_ref, wi_ref, bi_ref, sp_ref, o_ref,
                tail_ref, carry_ref, a_ref, u_ref, *, tc):
    c = pl.program_id(1)

    @pl.when(c == 0)
    def _():
        tail_ref[...] = jnp.zeros_like(tail_ref)
        carry_ref[...] = jnp.zeros_like(carry_ref)

    x = xr_ref[...].astype(f32)
    d = x.shape[1]
    row = lax.broadcasted_iota(jnp.int32, (tc, 1), 0)
    row8 = lax.broadcasted_iota(jnp.int32, (SUBLANES, 1), 0)
    tail = tail_ref[...]
    xc = cb_ref[...] + cw_ref[CONV_WIDTH - 1:CONV_WIDTH, :] * x
    for j in range(1, CONV_WIDTH):
        cur = pltpu.roll(x, j, 0)
        halo = pltpu.roll(tail, j, 0)
        head = jnp.where(row8 < j, halo, cur[0:SUBLANES])
        sh = jnp.concatenate([head, cur[SUBLANES:]], axis=0)
        xc = xc + cw_ref[CONV_WIDTH - 1 - j:CONV_WIDTH - j, :] * sh
    tail_ref[...] = x[tc - SUBLANES:]

    xcb = xc.astype(bf16)
    for s in range(d // RNN_SUPER):
        sl = slice(s * RNN_SUPER, (s + 1) * RNN_SUPER)
        za = jnp.dot(xcb[:, sl], wa_ref[s], preferred_element_type=f32) + ba_ref[:, sl]
        zi = jnp.dot(xcb[:, sl], wi_ref[s], preferred_element_type=f32) + bi_ref[:, sl]
        log_a = (-LRU_C) * _sigmoid(za) * sp_ref[:, sl]
        a = jnp.exp(log_a)
        u = jnp.sqrt(-jnp.tanh(log_a) * (a * a + 1.0)) * (_sigmoid(zi) * xc[:, sl])
        r8 = row % SUBLANES
        for k in (1, 2, 4):
            a_s = pltpu.roll(a, k, 0)
            u_s = pltpu.roll(u, k, 0)
            m = r8 >= k
            u = jnp.where(m, a * u_s + u, u)
            a = jnp.where(m, a * a_s, a)
        a_ref[:, sl] = a
        u_ref[:, sl] = u

    def group(g, h_prev):
        r0 = pl.multiple_of(g * SUBLANES, SUBLANES)
        h = a_ref[pl.ds(r0, SUBLANES), :] * h_prev + u_ref[pl.ds(r0, SUBLANES), :]
        u_ref[pl.ds(r0, SUBLANES), :] = h
        return jnp.broadcast_to(h[SUBLANES - 1:SUBLANES, :], h.shape)

    carry_ref[...] = lax.fori_loop(0, tc // SUBLANES, group, carry_ref[...])
    o_ref[...] = (u_ref[...] * _gelu(gr_ref[...].astype(f32))).astype(o_ref.dtype)


def _rglru(proj, B, S, d_rnn, conv_w, conv_b, wa, ba, wi, bi, sp, *, tc):
    nc = S // tc
    ns = d_rnn // RNN_SUPER
    row = lambda b, c: (b * nc + c, 0)
    const2 = lambda b, c: (0, 0)
    const3 = lambda b, c: (0, 0, 0)
    return pl.pallas_call(
        functools.partial(_rglru_body, tc=tc),
        grid=(B, nc),
        in_specs=[
            pl.BlockSpec((tc, d_rnn), row),
            pl.BlockSpec((tc, d_rnn), lambda b, c: (b * nc + c, 1)),
            pl.BlockSpec((CONV_WIDTH, d_rnn), const2),
            pl.BlockSpec((1, d_rnn), const2),
            pl.BlockSpec((ns, RNN_SUPER, RNN_SUPER), const3),
            pl.BlockSpec((1, d_rnn), const2),
            pl.BlockSpec((ns, RNN_SUPER, RNN_SUPER), const3),
            pl.BlockSpec((1, d_rnn), const2),
            pl.BlockSpec((1, d_rnn), const2),
        ],
        out_specs=pl.BlockSpec((tc, d_rnn), row),
        out_shape=jax.ShapeDtypeStruct((B * S, d_rnn), bf16),
        scratch_shapes=[pltpu.VMEM((SUBLANES, d_rnn), f32), pltpu.VMEM((SUBLANES, d_rnn), f32),
                        pltpu.VMEM((tc, d_rnn), f32), pltpu.VMEM((tc, d_rnn), f32)],
        compiler_params=_cparams(("parallel", "arbitrary")),
        name="rglru",
    )(proj, proj, conv_w, conv_b, wa, ba, wi, bi, sp)


def _super_blocks(w):
    nb, bd, _ = w.shape
    per = RNN_SUPER // bd
    out = jnp.zeros((nb // per, RNN_SUPER, RNN_SUPER), w.dtype)
    for k in range(per):
        out = out.at[:, k * bd:(k + 1) * bd, k * bd:(k + 1) * bd].set(w[k::per])
    return out.astype(bf16)


def _compress_body(x_ref, pos_ref, w1_ref, w2_ref, gain_ref, o_ref):
    kind = pl.program_id(1)
    x = x_ref[...].astype(f32)
    nrow = x.shape[0]
    half = x.shape[1]
    first = (x + pos_ref[:, :half]).astype(bf16)
    second = (x + pos_ref[:, half:]).astype(bf16)
    p = jnp.dot(first, w1_ref[:half, :], preferred_element_type=f32)
    q = jnp.dot(second, w1_ref[half:, :], preferred_element_type=f32)
    hid = _gelu(p + pltpu.roll(q, nrow - 1, 0))
    out = jnp.dot(hid.astype(bf16), w2_ref[...], preferred_element_type=f32)
    out = jnp.where(kind == 0, _rms(out, gain_ref[...]), out)
    rowi = lax.broadcasted_iota(jnp.int32, out.shape, 0)
    o_ref[...] = jnp.where(rowi < nrow - 1, out, 0.0).astype(o_ref.dtype)


def _compress(xk, pos, w1, w2, gain):
    B, _, G, nrow, width = xk.shape
    hid = w1.shape[-1]
    return pl.pallas_call(
        _compress_body,
        grid=(B, 2, G),
        in_specs=[
            pl.BlockSpec((None, None, None, nrow, width), lambda b, k, g: (b, k, g, 0, 0)),
            pl.BlockSpec((None, 1, 2 * width), lambda b, k, g: (k, 0, 0)),
            pl.BlockSpec((None, 2 * width, hid), lambda b, k, g: (k, 0, 0)),
            pl.BlockSpec((None, hid, HEAD_DIM), lambda b, k, g: (k, 0, 0)),
            pl.BlockSpec((1, HEAD_DIM), lambda b, k, g: (0, 0)),
        ],
        out_specs=pl.BlockSpec((None, None, None, nrow, HEAD_DIM), lambda b, k, g: (b, k, g, 0, 0)),
        out_shape=jax.ShapeDtypeStruct((B, 2, G, nrow, HEAD_DIM), bf16),
        compiler_params=_cparams(("parallel", "parallel", "parallel")),
        name="compress",
    )(xk, pos, w1, w2, gain)


def _head_gate(gl, col):
    lane = lax.broadcasted_iota(jnp.int32, gl.shape, 1)
    return _sigmoid(jnp.sum(jnp.where(lane == col, gl, 0.0), axis=1, keepdims=True))


def _slope_table():
    sl = 2.0 ** (-8.0 * np.arange(1, N_HEADS + 1) / N_HEADS)
    tab = np.zeros((N_KV, 1, LANES), np.float32)
    tab[:, 0, :HEADS_PER_KV] = sl.reshape(N_KV, HEADS_PER_KV)
    return jnp.asarray(tab)


def _cmp_attn_body(q_ref, kc_ref, vc_ref, qg_ref, gl_ref, ov_ref, sl_ref, oc_ref, sel_ref, *, tq, n_cmp):
    g = pl.program_id(1)
    t0 = pl.program_id(2) * tq
    q = q_ref[...].astype(f32)
    kc = kc_ref[...]
    vc = vc_ref[...]
    ncp = kc.shape[0]
    gl = gl_ref[...].astype(f32)
    t = t0 + lax.broadcasted_iota(jnp.int32, (tq, 1), 0)
    cidx = lax.broadcasted_iota(jnp.int32, (1, ncp), 1)
    cend = cidx * CMP_STRIDE + (CMP_BLOCK - 1)
    valid = (cend <= t) & (cidx < n_cmp)
    rel = (cend - t0).astype(f32)
    scale = HEAD_DIM ** -0.5
    imp = jnp.zeros((tq, LANES), f32)
    for h in range(HEADS_PER_KV):
        qh = (_rms(q[:, h * HEAD_DIM:(h + 1) * HEAD_DIM], qg_ref[...]) * scale).astype(bf16)
        s = lax.dot_general(qh, kc, (((1,), (1,)), ((), ())), preferred_element_type=f32)
        s = jnp.where(valid, s + sl_ref[:, h:h + 1] * rel, NEG)
        m = jnp.max(s, axis=1, keepdims=True)
        e = jnp.where(valid, jnp.exp(s - m), 0.0)
        den = jnp.sum(e, axis=1, keepdims=True)
        p = (e / jnp.where(den > 0.0, den, 1.0)).astype(bf16)
        o = jnp.dot(p, vc, preferred_element_type=f32)
        oc_ref[:, h * HEAD_DIM:(h + 1) * HEAD_DIM] = (o * _head_gate(gl, g * HEADS_PER_KV + h)).astype(oc_ref.dtype)
        imp = imp + jnp.dot(p, ov_ref[...], preferred_element_type=f32)

    n_sel = sel_ref.shape[1]
    j = lax.broadcasted_iota(jnp.int32, (1, LANES), 1)
    cur = t // SEL_BLOCK
    forced = (j == 0) | (j == cur) | (j == cur - 1)
    future = j * SEL_BLOCK > t
    imp = jnp.where(forced, FORCE, jnp.where(future, NEG, imp))
    imp = jnp.where(j < n_sel, imp, 2.0 * NEG)
    imp_t = imp.T[0:n_sel]
    row8 = lax.broadcasted_iota(jnp.int32, (SUBLANES, 1), 0)
    groups = [imp_t[r * SUBLANES:(r + 1) * SUBLANES] for r in range(n_sel // SUBLANES)]
    rows = []
    for jj in range(n_sel):
        v = imp_t[jj:jj + 1, :]
        part = jnp.zeros((SUBLANES, tq), f32)
        for r, blk in enumerate(groups):
            lo = r * SUBLANES
            if lo + SUBLANES <= jj:
                ahead = jnp.where(blk >= v, 1.0, 0.0)
            elif lo > jj:
                ahead = jnp.where(blk > v, 1.0, 0.0)
            else:
                ahead = jnp.where(row8 < jj - lo, jnp.where(blk >= v, 1.0, 0.0), jnp.where(blk > v, 1.0, 0.0))
            part = part + ahead
        rows.append(jnp.sum(part, axis=0, keepdims=True))
    rank = jnp.concatenate(rows, axis=0)
    sel_t = (rank < float(min(SEL_TOPK, n_sel))).astype(f32)
    if n_sel < LANES:
        sel_t = jnp.concatenate([sel_t, jnp.zeros((LANES - n_sel, tq), f32)], axis=0)
    sel_ref[...] = sel_t.T[:, 0:n_sel].astype(sel_ref.dtype)


def _cmp_attn(proj, gl, kvc, q_gain, overlap, B, S, q_col0, *, tq):
    G = N_KV
    ncp = kvc.shape[3]
    n_cmp = (S - CMP_BLOCK) // CMP_STRIDE + 1
    n_sel = S // SEL_BLOCK
    nq = S // tq
    gw = HEADS_PER_KV * HEAD_DIM
    qb0 = q_col0 // gw
    oc, sel = pl.pallas_call(
        functools.partial(_cmp_attn_body, tq=tq, n_cmp=n_cmp),
        grid=(B, G, nq),
        in_specs=[
            pl.BlockSpec((tq, gw), lambda b, g, i: (b * nq + i, qb0 + g)),
            pl.BlockSpec((None, None, None, ncp, HEAD_DIM), lambda b, g, i: (b, 0, g, 0, 0)),
            pl.BlockSpec((None, None, None, ncp, HEAD_DIM), lambda b, g, i: (b, 1, g, 0, 0)),
            pl.BlockSpec((1, HEAD_DIM), lambda b, g, i: (0, 0)),
            pl.BlockSpec((tq, LANES), lambda b, g, i: (b * nq + i, 0)),
            pl.BlockSpec((ncp, LANES), lambda b, g, i: (0, 0)),
            pl.BlockSpec((None, 1, LANES), lambda b, g, i: (g, 0, 0)),
        ],
        out_specs=[
            pl.BlockSpec((tq, gw), lambda b, g, i: (b * nq + i, g)),
            pl.BlockSpec((None, None, tq, n_sel), lambda b, g, i: (b, g, i, 0)),
        ],
        out_shape=[jax.ShapeDtypeStruct((B * S, N_HEADS * HEAD_DIM), bf16),
                   jax.ShapeDtypeStruct((B, G, S, n_sel), bf16)],
        compiler_params=_cparams(("parallel", "parallel", "parallel")),
        name="cmp_attn",
    )(proj, kvc, kvc, q_gain, gl, overlap, _slope_table())
    return oc, sel


def _sel_win_body(q_ref, ks_ref, vs_ref, kw_ref, vw_ref, qg_ref, kg_ref, gl_ref, sel_ref, oc_ref, sl_ref, o_ref,
                  ksn_ref, kwn_ref, m_ref, l_ref, acc_ref, *, tq, tk):
    g = pl.program_id(1)
    qi = pl.program_id(2)
    t0 = qi * tq
    hk = HEADS_PER_KV

    @pl.when(qi == 0)
    def _():
        ksn_ref[...] = _rms(ks_ref[...].astype(f32), kg_ref[0:1, :]).astype(bf16)
        kwn_ref[...] = _rms(kw_ref[...].astype(f32), kg_ref[1:2, :]).astype(bf16)

    q = q_ref[...].astype(f32)
    scale = HEAD_DIM ** -0.5
    qs = jnp.concatenate(
        [(_rms(q[:, h * HEAD_DIM:(h + 1) * HEAD_DIM], qg_ref[...]) * scale).astype(bf16) for h in range(hk)],
        axis=0)
    slope_col = jnp.concatenate([jnp.broadcast_to(sl_ref[:, h:h + 1], (tq, 1)) for h in range(hk)], axis=0)
    t = t0 + lax.broadcasted_iota(jnp.int32, (tq, 1), 0)
    kk = lax.broadcasted_iota(jnp.int32, (1, tk), 1)
    alibi = slope_col * kk.astype(f32)
    n_sel = sel_ref.shape[1]
    selm = sel_ref[...]
    blk_row = lax.broadcasted_iota(jnp.int32, (n_sel, tk), 0)
    blk_col = lax.broadcasted_iota(jnp.int32, (n_sel, tk), 1)

    def attend(k_ref, v_ref, lo, hi, valid_fn):
        m_ref[...] = jnp.full_like(m_ref, NEG)
        l_ref[...] = jnp.zeros_like(l_ref)
        acc_ref[...] = jnp.zeros_like(acc_ref)

        def step(kt, carry):
            k0 = pl.multiple_of(kt * tk, tk)
            k = k_ref[pl.ds(k0, tk), :]
            v = v_ref[pl.ds(k0, tk), :]
            mask = jnp.where(valid_fn(kt, k0 + kk), 0.0, NEG)
            s = lax.dot_general(qs, k, (((1,), (1,)), ((), ())), preferred_element_type=f32)
            s = s + (alibi + jnp.concatenate([mask] * hk, axis=0))
            shift = slope_col * (k0 - t0).astype(f32)
            m_old = m_ref[...]
            m_new = jnp.maximum(m_old, jnp.max(s, axis=1, keepdims=True) + shift)
            alpha = jnp.exp(m_old - m_new)
            p = jnp.exp(s - (m_new - shift))
            l_ref[...] = alpha * l_ref[...] + jnp.sum(p, axis=1, keepdims=True)
            acc_ref[...] = alpha * acc_ref[...] + jnp.dot(p.astype(bf16), v, preferred_element_type=f32)
            m_ref[...] = m_new
            return carry

        lax.fori_loop(lo, hi, step, 0)
        return acc_ref[...] / l_ref[...]

    def sel_valid(kt, kpos):
        expand = (blk_row == kt * (tk // SEL_BLOCK) + blk_col // SEL_BLOCK).astype(bf16)
        chosen = jnp.dot(selm, expand, preferred_element_type=f32)
        return (chosen > 0.5) & (kpos <= t)

    def win_valid(kt, kpos):
        return (kpos <= t) & (kpos > t - WINDOW)

    n_kt = (t0 + tq + tk - 1) // tk
    o_sel = attend(ksn_ref, vs_ref, 0, n_kt, sel_valid)
    w_lo = jnp.maximum(t0 - WINDOW + 1, 0) // tk
    o_win = attend(kwn_ref, vw_ref, w_lo, n_kt, win_valid)

    gl = gl_ref[...].astype(f32)
    for h in range(hk):
        col = g * hk + h
        rows = slice(h * tq, (h + 1) * tq)
        o = (oc_ref[:, h * HEAD_DIM:(h + 1) * HEAD_DIM].astype(f32)
             + _head_gate(gl, N_HEADS + col) * o_sel[rows] + _head_gate(gl, 2 * N_HEADS + col) * o_win[rows])
        o_ref[:, h * HEAD_DIM:(h + 1) * HEAD_DIM] = o.astype(o_ref.dtype)


def _sel_win_attn(proj, gl, sel, oc, q_gain, k_gains2, B, S, q_col0, kv_col0, *, tq, tk):
    G = N_KV
    nq = S // tq
    gw = HEADS_PER_KV * HEAD_DIM
    qb0 = q_col0 // gw
    kb0 = kv_col0 // HEAD_DIM
    n_sel = S // SEL_BLOCK

    def kv_spec(kind):
        return pl.BlockSpec((S, HEAD_DIM), lambda b, g, i, kind=kind: (b, kb0 + kind * G + g))

    return pl.pallas_call(
        functools.partial(_sel_win_body, tq=tq, tk=tk),
        grid=(B, G, nq),
        in_specs=[
            pl.BlockSpec((tq, gw), lambda b, g, i: (b * nq + i, qb0 + g)),
            kv_spec(2), kv_spec(3), kv_spec(4), kv_spec(5),
            pl.BlockSpec((1, HEAD_DIM), lambda b, g, i: (0, 0)),
            pl.BlockSpec((2, HEAD_DIM), lambda b, g, i: (0, 0)),
            pl.BlockSpec((tq, LANES), lambda b, g, i: (b * nq + i, 0)),
            pl.BlockSpec((None, None, tq, n_sel), lambda b, g, i: (b, g, i, 0)),
            pl.BlockSpec((tq, gw), lambda b, g, i: (b * nq + i, g)),
            pl.BlockSpec((None, 1, LANES), lambda b, g, i: (g, 0, 0)),
        ],
        out_specs=pl.BlockSpec((tq, gw), lambda b, g, i: (b * nq + i, g)),
        out_shape=jax.ShapeDtypeStruct((B * S, N_HEADS * HEAD_DIM), bf16),
        scratch_shapes=[pltpu.VMEM((S, HEAD_DIM), bf16), pltpu.VMEM((S, HEAD_DIM), bf16),
                        pltpu.VMEM((HEADS_PER_KV * tq, 1), f32), pltpu.VMEM((HEADS_PER_KV * tq, 1), f32),
                        pltpu.VMEM((HEADS_PER_KV * tq, HEAD_DIM), f32)],
        compiler_params=_cparams(("parallel", "parallel", "arbitrary")),
        name="sel_win_attn",
    )(proj, proj, proj, proj, proj, q_gain, k_gains2, gl, sel, oc, _slope_table())


def _router_body(x_ref, gain_ref, wr_ref, br_ref, xn_ref, info_ref, cnt_ref, carry_ref, *, tm, n_exp):
    @pl.when(pl.program_id(0) == 0)
    def _():
        carry_ref[...] = jnp.zeros_like(carry_ref)

    xn = _rms(x_ref[...], gain_ref[...])
    xn_ref[...] = xn
    lane = lax.broadcasted_iota(jnp.int32, (tm, LANES), 1)
    x_hi = xn.astype(bf16)
    x_lo = (xn - x_hi.astype(f32)).astype(bf16)
    logits = (jnp.dot(x_hi, wr_ref[0], preferred_element_type=f32)
              + jnp.dot(x_lo, wr_ref[0], preferred_element_type=f32)
              + jnp.dot(x_hi, wr_ref[1], preferred_element_type=f32)) + br_ref[...]
    logits = jnp.where(lane < n_exp, logits, 2.0 * NEG)
    hots, vals = [], []
    for _ in range(TOP_K):
        mx = jnp.max(logits, axis=1, keepdims=True)
        idx = jnp.min(jnp.where(logits == mx, lane, LANES), axis=1, keepdims=True)
        hot = lane == idx
        hots.append(hot)
        vals.append(mx)
        logits = jnp.where(hot, 2.0 * NEG, logits)
    es = [jnp.exp(v - vals[0]) for v in vals]
    den = es[0] + es[1] + es[2] + es[3]
    onehot = jnp.zeros((tm, LANES), f32)
    for hot in hots:
        onehot = onehot + hot.astype(f32)
    r = lax.broadcasted_iota(jnp.int32, (tm, tm), 0)
    c = lax.broadcasted_iota(jnp.int32, (tm, tm), 1)
    before = jnp.dot((c < r).astype(bf16), onehot.astype(bf16), preferred_element_type=f32)
    rank = carry_ref[...] + before
    carry_ref[...] = carry_ref[...] + jnp.sum(onehot, axis=0, keepdims=True)
    cnt_ref[...] = carry_ref[...]
    info = jnp.zeros((tm, LANES), f32)
    for k in range(TOP_K):
        e_k = jnp.sum(jnp.where(hots[k], lane, 0), axis=1, keepdims=True).astype(f32)
        r_k = jnp.sum(jnp.where(hots[k], rank, 0.0), axis=1, keepdims=True)
        info = jnp.where(lane == k, e_k, info)
        info = jnp.where(lane == TOP_K + k, es[k] / den, info)
        info = jnp.where(lane == 2 * TOP_K + k, r_k, info)
    info_ref[...] = info


def _router(x, gain, wr, br, *, tm):
    T, D = x.shape
    return pl.pallas_call(
        functools.partial(_router_body, tm=tm, n_exp=N_EXPERTS),
        grid=(T // tm,),
        in_specs=[
            pl.BlockSpec((tm, D), lambda i: (i, 0)),
            pl.BlockSpec((1, D), lambda i: (0, 0)),
            pl.BlockSpec((2, D, LANES), lambda i: (0, 0, 0)),
            pl.BlockSpec((1, LANES), lambda i: (0, 0)),
        ],
        out_specs=[
            pl.BlockSpec((tm, D), lambda i: (i, 0)),
            pl.BlockSpec((tm, LANES), lambda i: (i, 0)),
            pl.BlockSpec((1, LANES), lambda i: (0, 0)),
        ],
        out_shape=[jax.ShapeDtypeStruct((T, D), f32), jax.ShapeDtypeStruct((T, LANES), f32),
                   jax.ShapeDtypeStruct((1, LANES), f32)],
        scratch_shapes=[pltpu.VMEM((1, LANES), f32)],
        compiler_params=_cparams(("arbitrary",)),
        name="router",
    )(x, gain, wr, br)


def _dispatch_body(pos_ref, xn_ref, xs_in_ref, xs_ref, sem, *, tm):
    del xs_in_ref

    def row_copy(t, k):
        return pltpu.make_async_copy(xn_ref.at[pl.ds(t, 1)], xs_ref.at[pl.ds(pos_ref[t * TOP_K + k], 1)], sem)

    def issue(t, c):
        for k in range(TOP_K):
            row_copy(t, k).start()
        return c

    def drain(t, c):
        for k in range(TOP_K):
            row_copy(t, k).wait()
        return c

    lax.fori_loop(0, tm, issue, 0)
    lax.fori_loop(0, tm, drain, 0)


def _dispatch(pos_flat, xn, n_rows, *, tm):
    T, D = xn.shape
    xs0 = jnp.zeros((n_rows, D), xn.dtype)
    return pl.pallas_call(
        functools.partial(_dispatch_body, tm=tm),
        grid=(T // tm,),
        in_specs=[
            pl.BlockSpec((tm * TOP_K,), lambda i: (i,), memory_space=pltpu.SMEM),
            pl.BlockSpec((tm, D), lambda i: (i, 0)),
            pl.BlockSpec(memory_space=pl.ANY),
        ],
        out_specs=pl.BlockSpec(memory_space=pl.ANY),
        out_shape=jax.ShapeDtypeStruct((n_rows, D), xn.dtype),
        scratch_shapes=[pltpu.SemaphoreType.DMA],
        input_output_aliases={2: 0},
        compiler_params=_cparams(("arbitrary",), disable_bounds_checks=True, has_side_effects=True),
        name="moe_dispatch",
    )(pos_flat, xn, xs0)


def _up_body(be_ref, nu_ref, x_ref, wg_ref, wl_ref, bg_ref, bl_ref, h_ref):
    i = pl.program_id(1)

    @pl.when(i < nu_ref[0])
    def _():
        x = x_ref[...].astype(bf16)
        glu = jnp.minimum(jnp.dot(x, wg_ref[...], preferred_element_type=f32) + bg_ref[...], SWIGLU_LIMIT)
        lin = jnp.clip(jnp.dot(x, wl_ref[...], preferred_element_type=f32) + bl_ref[...], -SWIGLU_LIMIT,
                       SWIGLU_LIMIT)
        h_ref[...] = (glu * _sigmoid(SWIGLU_ALPHA * glu) * (lin + 1.0)).astype(h_ref.dtype)

    @pl.when(i >= nu_ref[0])
    def _():
        h_ref[...] = jnp.zeros_like(h_ref)


def _moe_up(block_e, n_used, xs, w_up, b_up, *, rb, tf):
    R, D = xs.shape
    d_ff = w_up.shape[2] // 2
    nj = d_ff // tf
    nb = R // rb
    rowmap = lambda j, i, be, nu: (jnp.minimum(i, nu[0] - 1), 0)
    return pl.pallas_call(
        _up_body,
        grid_spec=pltpu.PrefetchScalarGridSpec(
            num_scalar_prefetch=2,
            grid=(nj, nb),
            in_specs=[
                pl.BlockSpec((rb, D), rowmap),
                pl.BlockSpec((None, D, tf), lambda j, i, be, nu: (be[i], 0, j)),
                pl.BlockSpec((None, D, tf), lambda j, i, be, nu: (be[i], 0, nj + j)),
                pl.BlockSpec((None, 1, tf), lambda j, i, be, nu: (be[i], 0, j)),
                pl.BlockSpec((None, 1, tf), lambda j, i, be, nu: (be[i], 0, nj + j)),
            ],
            out_specs=pl.BlockSpec((rb, tf), lambda j, i, be, nu: (i, j)),
        ),
        out_shape=jax.ShapeDtypeStruct((R, d_ff), bf16),
        compiler_params=_cparams(("arbitrary", "arbitrary")),
        name="moe_up",
    )(block_e, n_used, xs, w_up, w_up, b_up, b_up)


def _down_body(be_ref, nu_ref, h_ref, w_ref, b_ref, y_ref):
    i = pl.program_id(1)

    @pl.when(i < nu_ref[0])
    def _():
        y_ref[...] = jnp.dot(h_ref[...], w_ref[...], preferred_element_type=f32) + b_ref[...]

    @pl.when(i >= nu_ref[0])
    def _():
        y_ref[...] = jnp.zeros_like(y_ref)


def _moe_down(block_e, n_used, h, w_down, b_down, *, rb, tn):
    R, F = h.shape
    D = w_down.shape[2]
    return pl.pallas_call(
        _down_body,
        grid_spec=pltpu.PrefetchScalarGridSpec(
            num_scalar_prefetch=2,
            grid=(D // tn, R // rb),
            in_specs=[
                pl.BlockSpec((rb, F), lambda j, i, be, nu: (jnp.minimum(i, nu[0] - 1), 0)),
                pl.BlockSpec((None, F, tn), lambda j, i, be, nu: (be[i], 0, j)),
                pl.BlockSpec((None, 1, tn), lambda j, i, be, nu: (be[i], 0, j)),
            ],
            out_specs=pl.BlockSpec((rb, tn), lambda j, i, be, nu: (i, j)),
        ),
        out_shape=jax.ShapeDtypeStruct((R, D), f32),
        compiler_params=_cparams(("arbitrary", "arbitrary")),
        name="moe_down",
    )(block_e, n_used, h, w_down, b_down)


def _combine_body(pos_ref, info_ref, x_ref, ys_ref, o_ref, buf_ref, sem, *, tm):
    def row_copy(t, k):
        return pltpu.make_async_copy(ys_ref.at[pl.ds(pos_ref[t * TOP_K + k], 1)], buf_ref.at[k, pl.ds(t, 1)], sem)

    def issue(t, c):
        for k in range(TOP_K):
            row_copy(t, k).start()
        return c

    def drain(t, c):
        for k in range(TOP_K):
            row_copy(t, k).wait()
        return c

    lax.fori_loop(0, tm, issue, 0)
    lax.fori_loop(0, tm, drain, 0)
    info = info_ref[...]
    lane = lax.broadcasted_iota(jnp.int32, info.shape, 1)
    y = x_ref[...]
    for k in range(TOP_K):
        gate = jnp.sum(jnp.where(lane == TOP_K + k, info, 0.0), axis=1, keepdims=True)
        y = y + gate * buf_ref[k]
    o_ref[...] = y


def _combine(pos_flat, info, x, ys, *, tm):
    T, D = x.shape
    return pl.pallas_call(
        functools.partial(_combine_body, tm=tm),
        grid=(T // tm,),
        in_specs=[
            pl.BlockSpec((tm * TOP_K,), lambda i: (i,), memory_space=pltpu.SMEM),
            pl.BlockSpec((tm, LANES), lambda i: (i, 0)),
            pl.BlockSpec((tm, D), lambda i: (i, 0)),
            pl.BlockSpec(memory_space=pl.ANY),
        ],
        out_specs=pl.BlockSpec((tm, D), lambda i: (i, 0)),
        out_shape=jax.ShapeDtypeStruct((T, D), f32),
        scratch_shapes=[pltpu.VMEM((TOP_K, tm, D), f32), pltpu.SemaphoreType.DMA],
        compiler_params=_cparams(("arbitrary",), disable_bounds_checks=True),
        name="moe_combine",
    )(pos_flat, info, x, ys)


def _pick(n, *cands):
    for c in cands:
        if n % c == 0:
            return c
    return n


def _layer(x2d, p2d, B, S, norm_mix, w_in, conv_w, conv_b, w_rg_a, b_rg_a, w_rg_i, b_rg_i, lru_lambda, w_rnn_out,
           q_gain, k_gains, cmp_pos_k, cmp_w1_k, cmp_w2_k, cmp_pos_v, cmp_w1_v, cmp_w2_v, w_nsa_out, w_out,
           norm_moe, w_router, b_router, w_up, b_up, w_down, b_down, w_ple_proj, ple_norm, w_ple_gate):
    T, D = x2d.shape
    d_rnn = conv_w.shape[1]
    q_w = N_HEADS * HEAD_DIM
    kv_w = 6 * N_KV * HEAD_DIM
    n_gate = 3 * N_HEADS
    q0 = 2 * d_rnn
    kv0 = q0 + q_w
    gn0 = kv0 + kv_w
    gm0 = gn0 + n_gate
    tm = _pick(T, 1024, 512, 256)

    w_main = jnp.concatenate([w_in[:, :gn0], w_in[:, gm0:]], axis=1).astype(bf16)
    w_gate = jnp.pad(w_in[:, gn0:gm0], ((0, 0), (0, LANES - n_gate))).astype(bf16)
    proj = _matmul(x2d, w_main, tm=tm, tn=1024, out_dtype=bf16, gain=norm_mix, name="in_proj")
    gl = _matmul(x2d, w_gate, tm=tm, tn=LANES, out_dtype=f32, gain=norm_mix, name="in_proj_gates")
    merge0 = gn0

    sp = jax.nn.softplus(-lru_lambda.astype(f32)).reshape(1, d_rnn)
    hg = _rglru(proj, B, S, d_rnn, conv_w, conv_b.reshape(1, d_rnn), _super_blocks(w_rg_a),
                b_rg_a.reshape(1, d_rnn), _super_blocks(w_rg_i), b_rg_i.reshape(1, d_rnn), sp,
                tc=_pick(S, 256, 128))

    nrow = S // CMP_STRIDE
    kv_cmp = proj[:, kv0:kv0 + 2 * N_KV * HEAD_DIM].reshape(B, nrow, CMP_STRIDE, 2, N_KV, HEAD_DIM)
    kv_cmp = kv_cmp.transpose(0, 3, 4, 1, 2, 5).reshape(B, 2, N_KV, nrow, CMP_STRIDE * HEAD_DIM)
    pos = jnp.stack([cmp_pos_k, cmp_pos_v]).reshape(2, 1, CMP_BLOCK * HEAD_DIM).astype(f32)
    w1 = jnp.stack([cmp_w1_k, cmp_w1_v]).astype(bf16)
    w2 = jnp.stack([cmp_w2_k, cmp_w2_v]).astype(bf16)
    kvc = _compress(kv_cmp, pos, w1, w2, k_gains[0:1].astype(f32))

    n_cmp = (S - CMP_BLOCK) // CMP_STRIDE + 1
    n_sel = S // SEL_BLOCK
    c0 = np.arange(nrow)[:, None] * CMP_STRIDE
    s0 = np.arange(LANES)[None, :] * SEL_BLOCK
    overlap = np.clip(np.minimum(c0 + CMP_BLOCK, s0 + SEL_BLOCK) - np.maximum(c0, s0), 0, None) / CMP_STRIDE
    overlap = overlap * (np.arange(nrow)[:, None] < n_cmp) * (np.arange(LANES)[None, :] < n_sel)
    qg = q_gain.reshape(1, HEAD_DIM).astype(f32)
    oc, sel = _cmp_attn(proj, gl, kvc, qg, jnp.asarray(overlap, bf16), B, S, q0, tq=_pick(S, 256, 128))
    o_nsa = _sel_win_attn(proj, gl, sel, oc, qg, k_gains[1:3].astype(f32), B, S, q0, kv0,
                          tq=_pick(S, 256, 128), tk=256)

    y_nsa = _matmul(o_nsa, w_nsa_out.astype(bf16), tm=tm, tn=1024, out_dtype=bf16,
                    extras=[(proj, merge0 + D)],
                    epilogue=lambda acc, m: _sigmoid(m.astype(f32)) * acc, name="nsa_out")
    merged = _matmul(hg, w_rnn_out.astype(bf16), tm=tm, tn=1024, out_dtype=bf16,
                     extras=[(proj, merge0), (y_nsa, 0)],
                     epilogue=lambda acc, m, y: _sigmoid(m.astype(f32)) * acc + y.astype(f32), name="rnn_out")
    x1 = _matmul(merged, w_out.astype(bf16), tm=tm, tn=1024, out_dtype=f32, extras=[(x2d, 0)],
                 epilogue=lambda acc, xr: xr + acc, name="mix_out")

    x2 = _moe(x1, norm_moe, w_router, b_router, w_up, b_up, w_down, b_down)

    ple_gain = ple_norm.reshape(1, D).astype(f32)
    ple = _ple_proj(p2d, w_ple_proj.astype(bf16), ple_gain, tm=_pick(T, 512, 256))
    return _matmul(x2, w_ple_gate.astype(bf16), tm=tm, tn=1024, out_dtype=f32, extras=[(x2, 0), (ple, 0)],
                   epilogue=lambda acc, xr, pe: xr + _sigmoid(acc) * pe.astype(f32), name="ple_gate")


def _moe(x1, norm_moe, w_router, b_router, w_up, b_up, w_down, b_down):
    T, D = x1.shape
    wr_f = jnp.pad(w_router.astype(f32), ((0, 0), (0, LANES - N_EXPERTS)))
    wr_hi = wr_f.astype(bf16)
    wr = jnp.stack([wr_hi, (wr_f - wr_hi.astype(f32)).astype(bf16)])
    br = jnp.pad(b_router, (0, LANES - N_EXPERTS)).reshape(1, LANES).astype(f32)
    xn, info, cnt = _router(x1, norm_moe.reshape(1, D).astype(f32), wr, br, tm=_pick(T, 512, 256))
    rb = MOE_ROWS
    eid = info[:, 0:TOP_K].astype(jnp.int32)
    rank = info[:, 2 * TOP_K:3 * TOP_K].astype(jnp.int32)
    counts = cnt[0, :N_EXPERTS].astype(jnp.int32)
    padded = (counts + rb - 1) // rb * rb
    pend = jnp.cumsum(padded)
    pstart = pend - padded
    pos_flat = (pstart[eid] + rank).reshape(T * TOP_K)
    n_blocks = (T * TOP_K) // rb + N_EXPERTS
    block_e = jnp.minimum(jnp.sum(jnp.arange(n_blocks)[:, None] * rb >= pend[None, :], axis=1),
                          N_EXPERTS - 1).astype(jnp.int32)
    n_used = (pend[-1:] // rb).astype(jnp.int32)
    xs = _dispatch(pos_flat, xn, n_blocks * rb, tm=_pick(T, 256))
    hmid = _moe_up(block_e, n_used, xs, w_up.astype(bf16), b_up.reshape(N_EXPERTS, 1, -1).astype(f32), rb=rb,
                   tf=1024)
    ys = _moe_down(block_e, n_used, hmid, w_down.astype(bf16), b_down.reshape(N_EXPERTS, 1, -1).astype(f32),
                   rb=rb, tn=1024)
    return _combine(pos_flat, info, x1, ys, tm=_pick(T, 256))


def _ple_body(p_ref, w_ref, g_ref, o_ref):
    acc = jnp.dot(p_ref[...].astype(bf16), w_ref[...], preferred_element_type=f32)
    o_ref[...] = _rms(acc, g_ref[...]).astype(o_ref.dtype)


def _ple_proj(p2d, w, gain, *, tm):
    T, K = p2d.shape
    D = w.shape[1]
    return pl.pallas_call(
        _ple_body,
        grid=(T // tm,),
        in_specs=[pl.BlockSpec((tm, K), lambda i: (i, 0)), pl.BlockSpec((K, D), lambda i: (0, 0)),
                  pl.BlockSpec((1, D), lambda i: (0, 0))],
        out_specs=pl.BlockSpec((tm, D), lambda i: (i, 0)),
        out_shape=jax.ShapeDtypeStruct((T, D), bf16),
        compiler_params=_cparams(("parallel",)),
        name="ple_proj",
    )(p2d, w, gain)


def kernel(x, p, norm_mix, w_in, conv_w, conv_b, w_rg_a, b_rg_a, w_rg_i, b_rg_i, lru_lambda, w_rnn_out, q_gain, k_gains, cmp_pos_k, cmp_w1_k, cmp_w2_k, cmp_pos_v, cmp_w1_v, cmp_w2_v, w_nsa_out, w_out, norm_moe, w_router, b_router, w_up, b_up, w_down, b_down, w_ple_proj, ple_norm, w_ple_gate):
    B, S, D = x.shape
    depth = p.shape[0]
    x2d = x.reshape(B * S, D)
    for i in range(depth):
        x2d = _layer(x2d, p[i].reshape(B * S, -1), B, S, norm_mix[i], w_in[i], conv_w[i], conv_b[i], w_rg_a[i],
                     b_rg_a[i], w_rg_i[i], b_rg_i[i], lru_lambda[i], w_rnn_out[i], q_gain[i], k_gains[i],
                     cmp_pos_k[i], cmp_w1_k[i], cmp_w2_k[i], cmp_pos_v[i], cmp_w1_v[i], cmp_w2_v[i], w_nsa_out[i],
                     w_out[i], norm_moe[i], w_router[i], b_router[i], w_up[i], b_up[i], w_down[i], b_down[i],
                     w_ple_proj[i], ple_norm[i], w_ple_gate[i])
    return x2d.reshape(B, S, D)
```

```python
import functools
import math

import numpy as np
import jax
import jax.numpy as jnp
from jax import lax
from jax.experimental import pallas as pl
from jax.experimental.pallas import tpu as pltpu

RNN_BLOCKS = 16
CONV_WIDTH = 4
LRU_C = 8.0
N_HEADS = 16
HEAD_DIM = 128
N_KV = 4
HEADS_PER_KV = N_HEADS // N_KV
CMP_BLOCK = 32
CMP_STRIDE = 16
SEL_BLOCK = 64
SEL_TOPK = 16
WINDOW = 512
N_EXPERTS = 32
TOP_K = 4
SWIGLU_LIMIT = 7.0
SWIGLU_ALPHA = 1.702
EPS = 1e-6
NEG = -1e30
FORCE = 1e9

LANES = 128
SUBLANES = 8
RNN_SUPER = 640
VMEM_LIMIT = 56 * 1024 * 1024
MOE_ROWS = 256
SEL_LANE0 = 64
N_SLOPE_PARTS = 3

bf16 = jnp.bfloat16
f32 = jnp.float32


def _cparams(sem, **kw):
    return pltpu.CompilerParams(dimension_semantics=sem, vmem_limit_bytes=VMEM_LIMIT, **kw)


def _gelu(x):
    return 0.5 * x * (1.0 + jnp.tanh(math.sqrt(2.0 / math.pi) * (x + 0.044715 * (x * x * x))))


def _sigmoid(x):
    return 1.0 / (1.0 + jnp.exp(-x))


def _rms(x, gain):
    return x * lax.rsqrt(jnp.mean(x * x, axis=-1, keepdims=True) + EPS) * gain


def _mm_body(*refs, has_gain, n_extra, epilogue, stage):
    i = 0
    a_ref = refs[i]; i += 1
    gain_ref = None
    if has_gain:
        gain_ref = refs[i]; i += 1
    w_ref = refs[i]; i += 1
    extra = refs[i:i + n_extra]; i += n_extra
    o_ref = refs[i]; i += 1
    if stage:
        h_ref = refs[i]

        @pl.when(pl.program_id(1) == 0)
        def _():
            a = a_ref[...].astype(f32)
            if has_gain:
                a = _rms(a, gain_ref[...])
            h_ref[...] = a.astype(bf16)

        lhs = h_ref[...]
    else:
        lhs = a_ref[...]
    acc = jnp.dot(lhs, w_ref[...], preferred_element_type=f32)
    o_ref[...] = epilogue(acc, *[e[...] for e in extra]).astype(o_ref.dtype)


def _matmul(a, w, *, tm, tn, out_dtype, gain=None, extras=(), epilogue=None, name="mm"):
    M, K = a.shape
    N = w.shape[1]
    assert M % tm == 0 and N % tn == 0
    stage = gain is not None or a.dtype != bf16
    if epilogue is None:
        epilogue = lambda acc: acc
    in_specs = [pl.BlockSpec((tm, K), lambda i, j: (i, 0))]
    args = [a]
    if gain is not None:
        in_specs.append(pl.BlockSpec((1, K), lambda i, j: (0, 0)))
        args.append(gain.reshape(1, K).astype(f32))
    in_specs.append(pl.BlockSpec((K, tn), lambda i, j: (0, j)))
    args.append(w)
    for arr, col0 in extras:
        assert col0 % tn == 0
        off = col0 // tn
        in_specs.append(pl.BlockSpec((tm, tn), lambda i, j, off=off: (i, j + off)))
        args.append(arr)
    body = functools.partial(_mm_body, has_gain=gain is not None, n_extra=len(extras), epilogue=epilogue,
                             stage=stage)
    return pl.pallas_call(
        body,
        grid=(M // tm, N // tn),
        in_specs=in_specs,
        out_specs=pl.BlockSpec((tm, tn), lambda i, j: (i, j)),
        out_shape=jax.ShapeDtypeStruct((M, N), out_dtype),
        scratch_shapes=[pltpu.VMEM((tm, K), bf16)] if stage else [],
        compiler_params=_cparams(("parallel", "arbitrary")),
        name=name,
    )(*args)


def _rglru_body(xr_ref, gr_ref, cw_ref, cb_ref, wa_ref, ba_ref, wi_ref, bi_ref, sp_ref, o_ref,
                tail_ref, carry_ref, a_ref, u_ref, *, tc):
    c = pl.program_id(1)

    @pl.when(c == 0)
    def _():
        tail_ref[...] = jnp.zeros_like(tail_ref)
        carry_ref[...] = jnp.zeros_like(carry_ref)

    x = xr_ref[...].astype(f32)
    d = x.shape[1]
    row = lax.broadcasted_iota(jnp.int32, (tc, 1), 0)
    row8 = lax.broadcasted_iota(jnp.int32, (SUBLANES, 1), 0)
    tail = tail_ref[...]
    xc = cb_ref[...] + cw_ref[CONV_WIDTH - 1:CONV_WIDTH, :] * x
    for j in range(1, CONV_WIDTH):
        cur = pltpu.roll(x, j, 0)
        halo = pltpu.roll(tail, j, 0)
        head = jnp.where(row8 < j, halo, cur[0:SUBLANES])
        sh = jnp.concatenate([head, cur[SUBLANES:]], axis=0)
        xc = xc + cw_ref[CONV_WIDTH - 1 - j:CONV_WIDTH - j, :] * sh
    tail_ref[...] = x[tc - SUBLANES:]

    xcb = xc.astype(bf16)
    for s in range(d // RNN_SUPER):
        sl = slice(s * RNN_SUPER, (s + 1) * RNN_SUPER)
        za = jnp.dot(xcb[:, sl], wa_ref[s], preferred_element_type=f32) + ba_ref[:, sl]
        zi = jnp.dot(xcb[:, sl], wi_ref[s], preferred_element_type=f32) + bi_ref[:, sl]
        log_a = (-LRU_C) * _sigmoid(za) * sp_ref[:, sl]
        a = jnp.exp(log_a)
        u = jnp.sqrt(-jnp.tanh(log_a) * (a * a + 1.0)) * (_sigmoid(zi) * xc[:, sl])
        r8 = row % SUBLANES
        for k in (1, 2, 4):
            a_s = pltpu.roll(a, k, 0)
            u_s = pltpu.roll(u, k, 0)
            m = r8 >= k
            u = jnp.where(m, a * u_s + u, u)
            a = jnp.where(m, a * a_s, a)
        a_ref[:, sl] = a
        u_ref[:, sl] = u

    def group(g, h_prev):
        r0 = pl.multiple_of(g * SUBLANES, SUBLANES)
        h = a_ref[pl.ds(r0, SUBLANES), :] * h_prev + u_ref[pl.ds(r0, SUBLANES), :]
        u_ref[pl.ds(r0, SUBLANES), :] = h
        return jnp.broadcast_to(h[SUBLANES - 1:SUBLANES, :], h.shape)

    carry_ref[...] = lax.fori_loop(0, tc // SUBLANES, group, carry_ref[...])
    o_ref[...] = (u_ref[...] * _gelu(gr_ref[...].astype(f32))).astype(o_ref.dtype)


def _rglru(proj, B, S, d_rnn, conv_w, conv_b, wa, ba, wi, bi, sp, *, tc):
    nc = S // tc
    ns = d_rnn // RNN_SUPER
    row = lambda b, c: (b * nc + c, 0)
    const2 = lambda b, c: (0, 0)
    const3 = lambda b, c: (0, 0, 0)
    return pl.pallas_call(
        functools.partial(_rglru_body, tc=tc),
        grid=(B, nc),
        in_specs=[
            pl.BlockSpec((tc, d_rnn), row),
            pl.BlockSpec((tc, d_rnn), lambda b, c: (b * nc + c, 1)),
            pl.BlockSpec((CONV_WIDTH, d_rnn), const2),
            pl.BlockSpec((1, d_rnn), const2),
            pl.BlockSpec((ns, RNN_SUPER, RNN_SUPER), const3),
            pl.BlockSpec((1, d_rnn), const2),
            pl.BlockSpec((ns, RNN_SUPER, RNN_SUPER), const3),
            pl.BlockSpec((1, d_rnn), const2),
            pl.BlockSpec((1, d_rnn), const2),
        ],
        out_specs=pl.BlockSpec((tc, d_rnn), row),
        out_shape=jax.ShapeDtypeStruct((B * S, d_rnn), bf16),
        scratch_shapes=[pltpu.VMEM((SUBLANES, d_rnn), f32), pltpu.VMEM((SUBLANES, d_rnn), f32),
                        pltpu.VMEM((tc, d_rnn), f32), pltpu.VMEM((tc, d_rnn), f32)],
        compiler_params=_cparams(("parallel", "arbitrary")),
        name="rglru",
    )(proj, proj, conv_w, conv_b, wa, ba, wi, bi, sp)


def _super_blocks(w):
    nb, bd, _ = w.shape
    per = RNN_SUPER // bd
    out = jnp.zeros((nb // per, RNN_SUPER, RNN_SUPER), w.dtype)
    for k in range(per):
        out = out.at[:, k * bd:(k + 1) * bd, k * bd:(k + 1) * bd].set(w[k::per])
    return out.astype(bf16)


def _compress_body(x_ref, pos_ref, w1_ref, w2_ref, gain_ref, o_ref):
    kind = pl.program_id(1)
    x = x_ref[...].astype(f32)
    nrow = x.shape[0]
    half = x.shape[1]
    first = (x + pos_ref[:, :half]).astype(bf16)
    second = (x + pos_ref[:, half:]).astype(bf16)
    p = jnp.dot(first, w1_ref[:half, :], preferred_element_type=f32)
    q = jnp.dot(second, w1_ref[half:, :], preferred_element_type=f32)
    hid = _gelu(p + pltpu.roll(q, nrow - 1, 0))
    out = jnp.dot(hid.astype(bf16), w2_ref[...], preferred_element_type=f32)
    out = jnp.where(kind == 0, _rms(out, gain_ref[...]), out)
    rowi = lax.broadcasted_iota(jnp.int32, out.shape, 0)
    o_ref[...] = jnp.where(rowi < nrow - 1, out, 0.0).astype(o_ref.dtype)


def _compress(xk, pos, w1, w2, gain):
    B, _, G, nrow, width = xk.shape
    hid = w1.shape[-1]
    return pl.pallas_call(
        _compress_body,
        grid=(B, 2, G),
        in_specs=[
            pl.BlockSpec((None, None, None, nrow, width), lambda b, k, g: (b, k, g, 0, 0)),
            pl.BlockSpec((None, 1, 2 * width), lambda b, k, g: (k, 0, 0)),
            pl.BlockSpec((None, 2 * width, hid), lambda b, k, g: (k, 0, 0)),
            pl.BlockSpec((None, hid, HEAD_DIM), lambda b, k, g: (k, 0, 0)),
            pl.BlockSpec((1, HEAD_DIM), lambda b, k, g: (0, 0)),
        ],
        out_specs=pl.BlockSpec((None, None, None, nrow, HEAD_DIM), lambda b, k, g: (b, k, g, 0, 0)),
        out_shape=jax.ShapeDtypeStruct((B, 2, G, nrow, HEAD_DIM), bf16),
        compiler_params=_cparams(("parallel", "parallel", "parallel")),
        name="compress",
    )(xk, pos, w1, w2, gain)


def _head_gate(gl, col):
    lane = lax.broadcasted_iota(jnp.int32, gl.shape, 1)
    return _sigmoid(jnp.sum(jnp.where(lane == col, gl, 0.0), axis=1, keepdims=True))


def _slope_table():
    sl = 2.0 ** (-8.0 * np.arange(1, N_HEADS + 1) / N_HEADS)
    tab = np.zeros((N_KV, 1, LANES), np.float32)
    tab[:, 0, :HEADS_PER_KV] = sl.reshape(N_KV, HEADS_PER_KV)
    return jnp.asarray(tab)


def _cmp_attn_body(q_ref, kc_ref, vc_ref, qg_ref, gl_ref, ov_ref, sl_ref, oc_ref, sel_ref, *, tq, n_cmp, n_sel):
    g = pl.program_id(1)
    t0 = pl.program_id(2) * tq
    q = q_ref[...].astype(f32)
    kc = kc_ref[...]
    vc = vc_ref[...]
    ncp = kc.shape[0]
    gl = gl_ref[...].astype(f32)
    t = t0 + lax.broadcasted_iota(jnp.int32, (tq, 1), 0)
    cidx = lax.broadcasted_iota(jnp.int32, (1, ncp), 1)
    cend = cidx * CMP_STRIDE + (CMP_BLOCK - 1)
    valid = (cend <= t) & (cidx < n_cmp)
    rel = (cend - t0).astype(f32)
    scale = HEAD_DIM ** -0.5
    imp = jnp.zeros((tq, LANES), f32)
    for h in range(HEADS_PER_KV):
        qh = (_rms(q[:, h * HEAD_DIM:(h + 1) * HEAD_DIM], qg_ref[...]) * scale).astype(bf16)
        s = lax.dot_general(qh, kc, (((1,), (1,)), ((), ())), preferred_element_type=f32)
        s = jnp.where(valid, s + sl_ref[:, h:h + 1] * rel, NEG)
        m = jnp.max(s, axis=1, keepdims=True)
        e = jnp.where(valid, jnp.exp(s - m), 0.0)
        den = jnp.sum(e, axis=1, keepdims=True)
        p = (e / jnp.where(den > 0.0, den, 1.0)).astype(bf16)
        o = jnp.dot(p, vc, preferred_element_type=f32)
        oc_ref[:, h * HEAD_DIM:(h + 1) * HEAD_DIM] = (o * _head_gate(gl, g * HEADS_PER_KV + h)).astype(oc_ref.dtype)
        imp = imp + jnp.dot(p, ov_ref[...], preferred_element_type=f32)

    j = lax.broadcasted_iota(jnp.int32, (1, LANES), 1)
    cur = t // SEL_BLOCK
    forced = (j == 0) | (j == cur) | (j == cur - 1)
    future = j * SEL_BLOCK > t
    imp = jnp.where(forced, FORCE, jnp.where(future, NEG, imp))
    imp = jnp.where(j < n_sel, imp, 2.0 * NEG)
    imp_t = imp.T[0:n_sel]
    row8 = lax.broadcasted_iota(jnp.int32, (SUBLANES, 1), 0)
    groups = [imp_t[r * SUBLANES:(r + 1) * SUBLANES] for r in range(n_sel // SUBLANES)]
    rows = []
    for jj in range(n_sel):
        v = imp_t[jj:jj + 1, :]
        part = jnp.zeros((SUBLANES, tq), f32)
        for r, blk in enumerate(groups):
            lo = r * SUBLANES
            if lo + SUBLANES <= jj:
                ahead = jnp.where(blk >= v, 1.0, 0.0)
            elif lo > jj:
                ahead = jnp.where(blk > v, 1.0, 0.0)
            else:
                ahead = jnp.where(row8 < jj - lo, jnp.where(blk >= v, 1.0, 0.0), jnp.where(blk > v, 1.0, 0.0))
            part = part + ahead
        rows.append(jnp.sum(part, axis=0, keepdims=True))
    rank = jnp.concatenate(rows, axis=0)
    neg_t = jnp.where(rank < float(min(SEL_TOPK, n_sel)), 0.0, NEG)
    pieces = [jnp.zeros((SEL_LANE0, tq), f32), neg_t]
    if SEL_LANE0 + n_sel < LANES:
        pieces.append(jnp.zeros((LANES - SEL_LANE0 - n_sel, tq), f32))
    sel_ref[...] = jnp.concatenate(pieces, axis=0).T.astype(sel_ref.dtype)


def _cmp_attn(proj, gl, kvc, q_gain, overlap, B, S, q_col0, *, tq):
    G = N_KV
    ncp = kvc.shape[3]
    n_cmp = (S - CMP_BLOCK) // CMP_STRIDE + 1
    n_sel = S // SEL_BLOCK
    nq = S // tq
    gw = HEADS_PER_KV * HEAD_DIM
    qb0 = q_col0 // gw
    assert SEL_LANE0 + n_sel <= LANES
    oc, sel = pl.pallas_call(
        functools.partial(_cmp_attn_body, tq=tq, n_cmp=n_cmp, n_sel=n_sel),
        grid=(B, G, nq),
        in_specs=[
            pl.BlockSpec((tq, gw), lambda b, g, i: (b * nq + i, qb0 + g)),
            pl.BlockSpec((None, None, None, ncp, HEAD_DIM), lambda b, g, i: (b, 0, g, 0, 0)),
            pl.BlockSpec((None, None, None, ncp, HEAD_DIM), lambda b, g, i: (b, 1, g, 0, 0)),
            pl.BlockSpec((1, HEAD_DIM), lambda b, g, i: (0, 0)),
            pl.BlockSpec((tq, LANES), lambda b, g, i: (b * nq + i, 0)),
            pl.BlockSpec((ncp, LANES), lambda b, g, i: (0, 0)),
            pl.BlockSpec((None, 1, LANES), lambda b, g, i: (g, 0, 0)),
        ],
        out_specs=[
            pl.BlockSpec((tq, gw), lambda b, g, i: (b * nq + i, g)),
            pl.BlockSpec((None, None, tq, LANES), lambda b, g, i: (b, g, i, 0)),
        ],
        out_shape=[jax.ShapeDtypeStruct((B * S, N_HEADS * HEAD_DIM), bf16),
                   jax.ShapeDtypeStruct((B, G, S, LANES), bf16)],
        compiler_params=_cparams(("parallel", "parallel", "parallel")),
        name="cmp_attn",
    )(proj, kvc, kvc, q_gain, gl, overlap, _slope_table())
    return oc, sel


def _sel_win_body(q_ref, ks_ref, vs_ref, kw_ref, vw_ref, qg_ref, kg_ref, gl_ref, sel_ref, oc_ref, sl_ref, o_ref,
                  ksa_ref, kwa_ref, s_ref, mx_ref, l_ref, acc_ref, *, tq):
    g = pl.program_id(1)
    qi = pl.program_id(2)
    hk = HEADS_PER_KV
    nw = WINDOW // tq
    nl = tq // LANES
    nt = (((1,), (1,)), ((), ()))

    @pl.when(qi == 0)
    def _():
        n_keys = ks_ref.shape[0]
        kpos = lax.broadcasted_iota(jnp.int32, (n_keys, LANES), 0)
        lane = lax.broadcasted_iota(jnp.int32, (n_keys, LANES), 1)
        blk = kpos // SEL_BLOCK
        pos_cols = jnp.where(lane < N_SLOPE_PARTS, blk.astype(f32),
                             jnp.where(lane < 2 * N_SLOPE_PARTS, (kpos % SEL_BLOCK).astype(f32), 0.0))
        onehot = jnp.where(lane - SEL_LANE0 == blk, 1.0, 0.0)
        ksa_ref[:, :HEAD_DIM] = _rms(ks_ref[...].astype(f32), kg_ref[0:1, :]).astype(bf16)
        ksa_ref[:, HEAD_DIM:] = (pos_cols + onehot).astype(bf16)
        kwa_ref[:, :HEAD_DIM] = _rms(kw_ref[...].astype(f32), kg_ref[1:2, :]).astype(bf16)
        kwa_ref[:, HEAD_DIM:] = pos_cols.astype(bf16)

    q = q_ref[...].astype(f32)
    scale = HEAD_DIM ** -0.5
    qx = sel_ref[...].astype(f32)
    qa = jnp.concatenate(
        [jnp.concatenate([(_rms(q[:, h * HEAD_DIM:(h + 1) * HEAD_DIM], qg_ref[...]) * scale).astype(bf16),
                          (qx + sl_ref[h:h + 1, :]).astype(bf16)], axis=1) for h in range(hk)],
        axis=0)
    r = lax.broadcasted_iota(jnp.int32, (tq, tq), 0)
    c = lax.broadcasted_iota(jnp.int32, (tq, tq), 1)
    mask_diag = jnp.concatenate([jnp.where(c <= r, 0.0, NEG)] * hk, axis=0)
    mask_edge = jnp.concatenate([jnp.where(c > r, 0.0, NEG)] * hk, axis=0)

    def reset():
        mx_ref[...] = jnp.full_like(mx_ref, NEG)
        l_ref[...] = jnp.zeros_like(l_ref)
        acc_ref[...] = jnp.zeros_like(acc_ref)

    def score_tile(ka_ref, kt, slot, mask):
        k0 = pl.multiple_of(kt * tq, tq)
        s = lax.dot_general(qa, ka_ref[pl.ds(k0, tq), :], nt, preferred_element_type=f32)
        if mask is not None:
            s = s + mask
        s_ref[slot] = s
        part = s[:, 0:LANES]
        for i in range(1, nl):
            part = jnp.maximum(part, s[:, i * LANES:(i + 1) * LANES])
        mx_ref[...] = jnp.maximum(mx_ref[...], part)

    def value_tile(v_ref, kt, slot, m):
        k0 = pl.multiple_of(kt * tq, tq)
        p = jnp.exp(s_ref[slot] - m)
        part = p[:, 0:LANES]
        for i in range(1, nl):
            part = part + p[:, i * LANES:(i + 1) * LANES]
        l_ref[...] += part
        acc_ref[...] += jnp.dot(p.astype(bf16), v_ref[pl.ds(k0, tq), :], preferred_element_type=f32)

    def finish():
        return acc_ref[...] / jnp.sum(l_ref[...], axis=1, keepdims=True)

    reset()

    def sel_scores(kt, carry):
        score_tile(ksa_ref, kt, kt, None)
        return carry

    lax.fori_loop(0, qi, sel_scores, 0)
    score_tile(ksa_ref, qi, qi, mask_diag)
    m_sel = jnp.max(mx_ref[...], axis=1, keepdims=True)

    def sel_values(kt, carry):
        value_tile(vs_ref, kt, kt, m_sel)
        return carry

    lax.fori_loop(0, qi + 1, sel_values, 0)
    o_sel = finish()

    reset()
    for d in range(nw, 0, -1):
        mask = mask_edge if d == nw else None
        pl.when(qi >= d)(functools.partial(score_tile, kwa_ref, qi - d, nw - d, mask))
    score_tile(kwa_ref, qi, nw, mask_diag)
    m_win = jnp.max(mx_ref[...], axis=1, keepdims=True)
    for d in range(nw, 0, -1):
        pl.when(qi >= d)(functools.partial(value_tile, vw_ref, qi - d, nw - d, m_win))
    value_tile(vw_ref, qi, nw, m_win)
    o_win = finish()

    gl = gl_ref[...].astype(f32)
    for h in range(hk):
        col = g * hk + h
        rows = slice(h * tq, (h + 1) * tq)
        o = (oc_ref[:, h * HEAD_DIM:(h + 1) * HEAD_DIM].astype(f32)
             + _head_gate(gl, N_HEADS + col) * o_sel[rows] + _head_gate(gl, 2 * N_HEADS + col) * o_win[rows])
        o_ref[:, h * HEAD_DIM:(h + 1) * HEAD_DIM] = o.astype(o_ref.dtype)


def _alibi_parts_table():
    sl = (2.0 ** (-8.0 * np.arange(1, N_HEADS + 1) / N_HEADS)).astype(np.float32)
    tab = np.zeros((N_HEADS, LANES), np.float32)
    rest = sl
    for i in range(N_SLOPE_PARTS):
        piece = rest.astype(bf16).astype(np.float32)
        rest = rest - piece
        tab[:, i] = piece * SEL_BLOCK
        tab[:, N_SLOPE_PARTS + i] = piece
    return jnp.asarray(tab.reshape(N_KV, HEADS_PER_KV, LANES))


def _sel_win_attn(proj, gl, sel, oc, q_gain, k_gains2, B, S, q_col0, kv_col0, *, tq):
    G = N_KV
    nq = S // tq
    gw = HEADS_PER_KV * HEAD_DIM
    qb0 = q_col0 // gw
    kb0 = kv_col0 // HEAD_DIM
    assert WINDOW % tq == 0 and tq % LANES == 0 and 2 * N_SLOPE_PARTS <= SEL_LANE0
    assert S // SEL_BLOCK <= 256
    n_slots = max(nq, WINDOW // tq + 1)

    def kv_spec(kind):
        return pl.BlockSpec((S, HEAD_DIM), lambda b, g, i, kind=kind: (b, kb0 + kind * G + g))

    return pl.pallas_call(
        functools.partial(_sel_win_body, tq=tq),
        grid=(B, G, nq),
        in_specs=[
            pl.BlockSpec((tq, gw), lambda b, g, i: (b * nq + i, qb0 + g)),
            kv_spec(2), kv_spec(3), kv_spec(4), kv_spec(5),
            pl.BlockSpec((1, HEAD_DIM), lambda b, g, i: (0, 0)),
            pl.BlockSpec((2, HEAD_DIM), lambda b, g, i: (0, 0)),
            pl.BlockSpec((tq, LANES), lambda b, g, i: (b * nq + i, 0)),
            pl.BlockSpec((None, None, tq, LANES), lambda b, g, i: (b, g, i, 0)),
            pl.BlockSpec((tq, gw), lambda b, g, i: (b * nq + i, g)),
            pl.BlockSpec((None, HEADS_PER_KV, LANES), lambda b, g, i: (g, 0, 0)),
        ],
        out_specs=pl.BlockSpec((tq, gw), lambda b, g, i: (b * nq + i, g)),
        out_shape=jax.ShapeDtypeStruct((B * S, N_HEADS * HEAD_DIM), bf16),
        scratch_shapes=[pltpu.VMEM((S, 2 * HEAD_DIM), bf16), pltpu.VMEM((S, 2 * HEAD_DIM), bf16),
                        pltpu.VMEM((n_slots, HEADS_PER_KV * tq, tq), f32),
                        pltpu.VMEM((HEADS_PER_KV * tq, LANES), f32), pltpu.VMEM((HEADS_PER_KV * tq, LANES), f32),
                        pltpu.VMEM((HEADS_PER_KV * tq, HEAD_DIM), f32)],
        compiler_params=_cparams(("parallel", "parallel", "arbitrary")),
        name="sel_win_attn",
    )(proj, proj, proj, proj, proj, q_gain, k_gains2, gl, sel, oc, _alibi_parts_table())


def _router_body(x_ref, gain_ref, wr_ref, br_ref, xn_ref, info_ref, cnt_ref, carry_ref, *, tm, n_exp):
    @pl.when(pl.program_id(0) == 0)
    def _():
        carry_ref[...] = jnp.zeros_like(carry_ref)

    xn = _rms(x_ref[...], gain_ref[...])
    xn_ref[...] = xn
    lane = lax.broadcasted_iota(jnp.int32, (tm, LANES), 1)
    x_hi = xn.astype(bf16)
    x_lo = (xn - x_hi.astype(f32)).astype(bf16)
    logits = (jnp.dot(x_hi, wr_ref[0], preferred_element_type=f32)
              + jnp.dot(x_lo, wr_ref[0], preferred_element_type=f32)
              + jnp.dot(x_hi, wr_ref[1], preferred_element_type=f32)) + br_ref[...]
    logits = jnp.where(lane < n_exp, logits, 2.0 * NEG)
    hots, vals = [], []
    for _ in range(TOP_K):
        mx = jnp.max(logits, axis=1, keepdims=True)
        idx = jnp.min(jnp.where(logits == mx, lane, LANES), axis=1, keepdims=True)
        hot = lane == idx
        hots.append(hot)
        vals.append(mx)
        logits = jnp.where(hot, 2.0 * NEG, logits)
    es = [jnp.exp(v - vals[0]) for v in vals]
    den = es[0] + es[1] + es[2] + es[3]
    onehot = jnp.zeros((tm, LANES), f32)
    for hot in hots:
        onehot = onehot + hot.astype(f32)
    r = lax.broadcasted_iota(jnp.int32, (tm, tm), 0)
    c = lax.broadcasted_iota(jnp.int32, (tm, tm), 1)
    before = jnp.dot((c < r).astype(bf16), onehot.astype(bf16), preferred_element_type=f32)
    rank = carry_ref[...] + before
    carry_ref[...] = carry_ref[...] + jnp.sum(onehot, axis=0, keepdims=True)
    cnt_ref[...] = carry_ref[...]
    info = jnp.zeros((tm, LANES), f32)
    for k in range(TOP_K):
        e_k = jnp.sum(jnp.where(hots[k], lane, 0), axis=1, keepdims=True).astype(f32)
        r_k = jnp.sum(jnp.where(hots[k], rank, 0.0), axis=1, keepdims=True)
        info = jnp.where(lane == k, e_k, info)
        info = jnp.where(lane == TOP_K + k, es[k] / den, info)
        info = jnp.where(lane == 2 * TOP_K + k, r_k, info)
    info_ref[...] = info


def _router(x, gain, wr, br, *, tm):
    T, D = x.shape
    return pl.pallas_call(
        functools.partial(_router_body, tm=tm, n_exp=N_EXPERTS),
        grid=(T // tm,),
        in_specs=[
            pl.BlockSpec((tm, D), lambda i: (i, 0)),
            pl.BlockSpec((1, D), lambda i: (0, 0)),
            pl.BlockSpec((2, D, LANES), lambda i: (0, 0, 0)),
            pl.BlockSpec((1, LANES), lambda i: (0, 0)),
        ],
        out_specs=[
            pl.BlockSpec((tm, D), lambda i: (i, 0)),
            pl.BlockSpec((tm, LANES), lambda i: (i, 0)),
            pl.BlockSpec((1, LANES), lambda i: (0, 0)),
        ],
        out_shape=[jax.ShapeDtypeStruct((T, D), f32), jax.ShapeDtypeStruct((T, LANES), f32),
                   jax.ShapeDtypeStruct((1, LANES), f32)],
        scratch_shapes=[pltpu.VMEM((1, LANES), f32)],
        compiler_params=_cparams(("arbitrary",)),
        name="router",
    )(x, gain, wr, br)


def _dispatch_body(pos_ref, xn_ref, xs_in_ref, xs_ref, sem, *, tm):
    del xs_in_ref

    def row_copy(t, k):
        return pltpu.make_async_copy(xn_ref.at[pl.ds(t, 1)], xs_ref.at[pl.ds(pos_ref[t * TOP_K + k], 1)], sem)

    def issue(t, c):
        for k in range(TOP_K):
            row_copy(t, k).start()
        return c

    def drain(t, c):
        for k in range(TOP_K):
            row_copy(t, k).wait()
        return c

    lax.fori_loop(0, tm, issue, 0)
    lax.fori_loop(0, tm, drain, 0)


def _dispatch(pos_flat, xn, n_rows, *, tm):
    T, D = xn.shape
    xs0 = jnp.zeros((n_rows, D), xn.dtype)
    return pl.pallas_call(
        functools.partial(_dispatch_body, tm=tm),
        grid=(T // tm,),
        in_specs=[
            pl.BlockSpec((tm * TOP_K,), lambda i: (i,), memory_space=pltpu.SMEM),
            pl.BlockSpec((tm, D), lambda i: (i, 0)),
            pl.BlockSpec(memory_space=pl.ANY),
        ],
        out_specs=pl.BlockSpec(memory_space=pl.ANY),
        out_shape=jax.ShapeDtypeStruct((n_rows, D), xn.dtype),
        scratch_shapes=[pltpu.SemaphoreType.DMA],
        input_output_aliases={2: 0},
        compiler_params=_cparams(("arbitrary",), disable_bounds_checks=True, has_side_effects=True),
        name="moe_dispatch",
    )(pos_flat, xn, xs0)


def _up_body(be_ref, nu_ref, x_ref, wg_ref, wl_ref, bg_ref, bl_ref, h_ref):
    i = pl.program_id(1)

    @pl.when(i < nu_ref[0])
    def _():
        x = x_ref[...].astype(bf16)
        glu = jnp.minimum(jnp.dot(x, wg_ref[...], preferred_element_type=f32) + bg_ref[...], SWIGLU_LIMIT)
        lin = jnp.clip(jnp.dot(x, wl_ref[...], preferred_element_type=f32) + bl_ref[...], -SWIGLU_LIMIT,
                       SWIGLU_LIMIT)
        h_ref[...] = (glu * _sigmoid(SWIGLU_ALPHA * glu) * (lin + 1.0)).astype(h_ref.dtype)

    @pl.when(i >= nu_ref[0])
    def _():
        h_ref[...] = jnp.zeros_like(h_ref)


def _moe_up(block_e, n_used, xs, w_up, b_up, *, rb, tf):
    R, D = xs.shape
    d_ff = w_up.shape[2] // 2
    nj = d_ff // tf
    nb = R // rb
    rowmap = lambda j, i, be, nu: (jnp.minimum(i, nu[0] - 1), 0)
    return pl.pallas_call(
        _up_body,
        grid_spec=pltpu.PrefetchScalarGridSpec(
            num_scalar_prefetch=2,
            grid=(nj, nb),
            in_specs=[
                pl.BlockSpec((rb, D), rowmap),
                pl.BlockSpec((None, D, tf), lambda j, i, be, nu: (be[i], 0, j)),
                pl.BlockSpec((None, D, tf), lambda j, i, be, nu: (be[i], 0, nj + j)),
                pl.BlockSpec((None, 1, tf), lambda j, i, be, nu: (be[i], 0, j)),
                pl.BlockSpec((None, 1, tf), lambda j, i, be, nu: (be[i], 0, nj + j)),
            ],
            out_specs=pl.BlockSpec((rb, tf), lambda j, i, be, nu: (i, j)),
        ),
        out_shape=jax.ShapeDtypeStruct((R, d_ff), bf16),
        compiler_params=_cparams(("arbitrary", "arbitrary")),
        name="moe_up",
    )(block_e, n_used, xs, w_up, w_up, b_up, b_up)


def _down_body(be_ref, nu_ref, h_ref, w_ref, b_ref, y_ref):
    i = pl.program_id(1)

    @pl.when(i < nu_ref[0])
    def _():
        y_ref[...] = jnp.dot(h_ref[...], w_ref[...], preferred_element_type=f32) + b_ref[...]

    @pl.when(i >= nu_ref[0])
    def _():
        y_ref[...] = jnp.zeros_like(y_ref)


def _moe_down(block_e, n_used, h, w_down, b_down, *, rb, tn):
    R, F = h.shape
    D = w_down.shape[2]
    return pl.pallas_call(
        _down_body,
        grid_spec=pltpu.PrefetchScalarGridSpec(
            num_scalar_prefetch=2,
            grid=(D // tn, R // rb),
            in_specs=[
                pl.BlockSpec((rb, F), lambda j, i, be, nu: (jnp.minimum(i, nu[0] - 1), 0)),
                pl.BlockSpec((None, F, tn), lambda j, i, be, nu: (be[i], 0, j)),
                pl.BlockSpec((None, 1, tn), lambda j, i, be, nu: (be[i], 0, j)),
            ],
            out_specs=pl.BlockSpec((rb, tn), lambda j, i, be, nu: (i, j)),
        ),
        out_shape=jax.ShapeDtypeStruct((R, D), f32),
        compiler_params=_cparams(("arbitrary", "arbitrary")),
        name="moe_down",
    )(block_e, n_used, h, w_down, b_down)


def _combine_body(pos_ref, info_ref, x_ref, ys_ref, o_ref, buf_ref, sem, *, tm):
    def row_copy(t, k):
        return pltpu.make_async_copy(ys_ref.at[pl.ds(pos_ref[t * TOP_K + k], 1)], buf_ref.at[k, pl.ds(t, 1)], sem)

    def issue(t, c):
        for k in range(TOP_K):
            row_copy(t, k).start()
        return c

    def drain(t, c):
        for k in range(TOP_K):
            row_copy(t, k).wait()
        return c

    lax.fori_loop(0, tm, issue, 0)
    lax.fori_loop(0, tm, drain, 0)
    info = info_ref[...]
    lane = lax.broadcasted_iota(jnp.int32, info.shape, 1)
    y = x_ref[...]
    for k in range(TOP_K):
        gate = jnp.sum(jnp.where(lane == TOP_K + k, info, 0.0), axis=1, keepdims=True)
        y = y + gate * buf_ref[k]
    o_ref[...] = y


def _combine(pos_flat, info, x, ys, *, tm):
    T, D = x.shape
    return pl.pallas_call(
        functools.partial(_combine_body, tm=tm),
        grid=(T // tm,),
        in_specs=[
            pl.BlockSpec((tm * TOP_K,), lambda i: (i,), memory_space=pltpu.SMEM),
            pl.BlockSpec((tm, LANES), lambda i: (i, 0)),
            pl.BlockSpec((tm, D), lambda i: (i, 0)),
            pl.BlockSpec(memory_space=pl.ANY),
        ],
        out_specs=pl.BlockSpec((tm, D), lambda i: (i, 0)),
        out_shape=jax.ShapeDtypeStruct((T, D), f32),
        scratch_shapes=[pltpu.VMEM((TOP_K, tm, D), f32), pltpu.SemaphoreType.DMA],
        compiler_params=_cparams(("arbitrary",), disable_bounds_checks=True),
        name="moe_combine",
    )(pos_flat, info, x, ys)


def _pick(n, *cands):
    for c in cands:
        if n % c == 0:
            return c
    return n


def _layer(x2d, p2d, B, S, norm_mix, w_in, conv_w, conv_b, w_rg_a, b_rg_a, w_rg_i, b_rg_i, lru_lambda, w_rnn_out,
           q_gain, k_gains, cmp_pos_k, cmp_w1_k, cmp_w2_k, cmp_pos_v, cmp_w1_v, cmp_w2_v, w_nsa_out, w_out,
           norm_moe, w_router, b_router, w_up, b_up, w_down, b_down, w_ple_proj, ple_norm, w_ple_gate):
    T, D = x2d.shape
    d_rnn = conv_w.shape[1]
    q_w = N_HEADS * HEAD_DIM
    kv_w = 6 * N_KV * HEAD_DIM
    n_gate = 3 * N_HEADS
    q0 = 2 * d_rnn
    kv0 = q0 + q_w
    gn0 = kv0 + kv_w
    gm0 = gn0 + n_gate
    tm = _pick(T, 1024, 512, 256)

    w_main = jnp.concatenate([w_in[:, :gn0], w_in[:, gm0:]], axis=1).astype(bf16)
    w_gate = jnp.pad(w_in[:, gn0:gm0], ((0, 0), (0, LANES - n_gate))).astype(bf16)
    proj = _matmul(x2d, w_main, tm=tm, tn=1024, out_dtype=bf16, gain=norm_mix, name="in_proj")
    gl = _matmul(x2d, w_gate, tm=tm, tn=LANES, out_dtype=f32, gain=norm_mix, name="in_proj_gates")
    merge0 = gn0

    sp = jax.nn.softplus(-lru_lambda.astype(f32)).reshape(1, d_rnn)
    hg = _rglru(proj, B, S, d_rnn, conv_w, conv_b.reshape(1, d_rnn), _super_blocks(w_rg_a),
                b_rg_a.reshape(1, d_rnn), _super_blocks(w_rg_i), b_rg_i.reshape(1, d_rnn), sp,
                tc=_pick(S, 256, 128))

    nrow = S // CMP_STRIDE
    kv_cmp = proj[:, kv0:kv0 + 2 * N_KV * HEAD_DIM].reshape(B, nrow, CMP_STRIDE, 2, N_KV, HEAD_DIM)
    kv_cmp = kv_cmp.transpose(0, 3, 4, 1, 2, 5).reshape(B, 2, N_KV, nrow, CMP_STRIDE * HEAD_DIM)
    pos = jnp.stack([cmp_pos_k, cmp_pos_v]).reshape(2, 1, CMP_BLOCK * HEAD_DIM).astype(f32)
    w1 = jnp.stack([cmp_w1_k, cmp_w1_v]).astype(bf16)
    w2 = jnp.stack([cmp_w2_k, cmp_w2_v]).astype(bf16)
    kvc = _compress(kv_cmp, pos, w1, w2, k_gains[0:1].astype(f32))

    n_cmp = (S - CMP_BLOCK) // CMP_STRIDE + 1
    n_sel = S // SEL_BLOCK
    c0 = np.arange(nrow)[:, None] * CMP_STRIDE
    s0 = np.arange(LANES)[None, :] * SEL_BLOCK
    overlap = np.clip(np.minimum(c0 + CMP_BLOCK, s0 + SEL_BLOCK) - np.maximum(c0, s0), 0, None) / CMP_STRIDE
    overlap = overlap * (np.arange(nrow)[:, None] < n_cmp) * (np.arange(LANES)[None, :] < n_sel)
    qg = q_gain.reshape(1, HEAD_DIM).astype(f32)
    oc, sel = _cmp_attn(proj, gl, kvc, qg, jnp.asarray(overlap, bf16), B, S, q0, tq=_pick(S, 256, 128))
    o_nsa = _sel_win_attn(proj, gl, sel, oc, qg, k_gains[1:3].astype(f32), B, S, q0, kv0,
                          tq=_pick(S, 256, 128))

    y_nsa = _matmul(o_nsa, w_nsa_out.astype(bf16), tm=tm, tn=1024, out_dtype=bf16,
                    extras=[(proj, merge0 + D)],
                    epilogue=lambda acc, m: _sigmoid(m.astype(f32)) * acc, name="nsa_out")
    merged = _matmul(hg, w_rnn_out.astype(bf16), tm=tm, tn=1024, out_dtype=bf16,
                     extras=[(proj, merge0), (y_nsa, 0)],
                     epilogue=lambda acc, m, y: _sigmoid(m.astype(f32)) * acc + y.astype(f32), name="rnn_out")
    x1 = _matmul(merged, w_out.astype(bf16), tm=tm, tn=1024, out_dtype=f32, extras=[(x2d, 0)],
                 epilogue=lambda acc, xr: xr + acc, name="mix_out")

    x2 = _moe(x1, norm_moe, w_router, b_router, w_up, b_up, w_down, b_down)

    ple_gain = ple_norm.reshape(1, D).astype(f32)
    ple = _ple_proj(p2d, w_ple_proj.astype(bf16), ple_gain, tm=_pick(T, 512, 256))
    return _matmul(x2, w_ple_gate.astype(bf16), tm=tm, tn=1024, out_dtype=f32, extras=[(x2, 0), (ple, 0)],
                   epilogue=lambda acc, xr, pe: xr + _sigmoid(acc) * pe.astype(f32), name="ple_gate")


def _moe(x1, norm_moe, w_router, b_router, w_up, b_up, w_down, b_down):
    T, D = x1.shape
    wr_f = jnp.pad(w_router.astype(f32), ((0, 0), (0, LANES - N_EXPERTS)))
    wr_hi = wr_f.astype(bf16)
    wr = jnp.stack([wr_hi, (wr_f - wr_hi.astype(f32)).astype(bf16)])
    br = jnp.pad(b_router, (0, LANES - N_EXPERTS)).reshape(1, LANES).astype(f32)
    xn, info, cnt = _router(x1, norm_moe.reshape(1, D).astype(f32), wr, br, tm=_pick(T, 512, 256))
    rb = MOE_ROWS
    eid = info[:, 0:TOP_K].astype(jnp.int32)
    rank = info[:, 2 * TOP_K:3 * TOP_K].astype(jnp.int32)
    counts = cnt[0, :N_EXPERTS].astype(jnp.int32)
    padded = (counts + rb - 1) // rb * rb
    pend = jnp.cumsum(padded)
    pstart = pend - padded
    pos_flat = (pstart[eid] + rank).reshape(T * TOP_K)
    n_blocks = (T * TOP_K) // rb + N_EXPERTS
    block_e = jnp.minimum(jnp.sum(jnp.arange(n_blocks)[:, None] * rb >= pend[None, :], axis=1),
                          N_EXPERTS - 1).astype(jnp.int32)
    n_used = (pend[-1:] // rb).astype(jnp.int32)
    xs = _dispatch(pos_flat, xn, n_blocks * rb, tm=_pick(T, 256))
    hmid = _moe_up(block_e, n_used, xs, w_up.astype(bf16), b_up.reshape(N_EXPERTS, 1, -1).astype(f32), rb=rb,
                   tf=1024)
    ys = _moe_down(block_e, n_used, hmid, w_down.astype(bf16), b_down.reshape(N_EXPERTS, 1, -1).astype(f32),
                   rb=rb, tn=1024)
    return _combine(pos_flat, info, x1, ys, tm=_pick(T, 256))


def _ple_body(p_ref, w_ref, g_ref, o_ref):
    acc = jnp.dot(p_ref[...].astype(bf16), w_ref[...], preferred_element_type=f32)
    o_ref[...] = _rms(acc, g_ref[...]).astype(o_ref.dtype)


def _ple_proj(p2d, w, gain, *, tm):
    T, K = p2d.shape
    D = w.shape[1]
    return pl.pallas_call(
        _ple_body,
        grid=(T // tm,),
        in_specs=[pl.BlockSpec((tm, K), lambda i: (i, 0)), pl.BlockSpec((K, D), lambda i: (0, 0)),
                  pl.BlockSpec((1, D), lambda i: (0, 0))],
        out_specs=pl.BlockSpec((tm, D), lambda i: (i, 0)),
        out_shape=jax.ShapeDtypeStruct((T, D), bf16),
        compiler_params=_cparams(("parallel",)),
        name="ple_proj",
    )(p2d, w, gain)


def kernel(x, p, norm_mix, w_in, conv_w, conv_b, w_rg_a, b_rg_a, w_rg_i, b_rg_i, lru_lambda, w_rnn_out, q_gain, k_gains, cmp_pos_k, cmp_w1_k, cmp_w2_k, cmp_pos_v, cmp_w1_v, cmp_w2_v, w_nsa_out, w_out, norm_moe, w_router, b_router, w_up, b_up, w_down, b_down, w_ple_proj, ple_norm, w_ple_gate):
    B, S, D = x.shape
    depth = p.shape[0]
    x2d = x.reshape(B * S, D)
    for i in range(depth):
        x2d = _layer(x2d, p[i].reshape(B * S, -1), B, S, norm_mix[i], w_in[i], conv_w[i], conv_b[i], w_rg_a[i],
                     b_rg_a[i], w_rg_i[i], b_rg_i[i], lru_lambda[i], w_rnn_out[i], q_gain[i], k_gains[i],
                     cmp_pos_k[i], cmp_w1_k[i], cmp_w2_k[i], cmp_pos_v[i], cmp_w1_v[i], cmp_w2_v[i], w_nsa_out[i],
                     w_out[i], norm_moe[i], w_router[i], b_router[i], w_up[i], b_up[i], w_down[i], b_down[i],
                     w_ple_proj[i], ple_norm[i], w_ple_gate[i])
    return x2d.reshape(B, S, D)
```

```python
import functools
import math

import numpy as np
import jax
import jax.numpy as jnp
from jax import lax
from jax.experimental import pallas as pl
from jax.experimental.pallas import tpu as pltpu

RNN_BLOCKS = 16
CONV_WIDTH = 4
LRU_C = 8.0
N_HEADS = 16
HEAD_DIM = 128
N_KV = 4
HEADS_PER_KV = N_HEADS // N_KV
CMP_BLOCK = 32
CMP_STRIDE = 16
SEL_BLOCK = 64
SEL_TOPK = 16
WINDOW = 512
N_EXPERTS = 32
TOP_K = 4
SWIGLU_LIMIT = 7.0
SWIGLU_ALPHA = 1.702
EPS = 1e-6
NEG = -1e30
FORCE = 1e9

LANES = 128
SUBLANES = 8
RNN_SUPER = 640
VMEM_LIMIT = 56 * 1024 * 1024
MOE_VMEM_LIMIT = 60 * 1024 * 1024
MOE_ROWS = 256
SEL_LANE0 = 64
N_SLOPE_PARTS = 3

bf16 = jnp.bfloat16
f32 = jnp.float32


def _cparams(sem, **kw):
    return pltpu.CompilerParams(dimension_semantics=sem, vmem_limit_bytes=VMEM_LIMIT, **kw)


def _gelu(x):
    return 0.5 * x * (1.0 + jnp.tanh(math.sqrt(2.0 / math.pi) * (x + 0.044715 * (x * x * x))))


def _sigmoid(x):
    return 0.5 * jnp.tanh(0.5 * x) + 0.5


def _rms(x, gain):
    return x * lax.rsqrt(jnp.mean(x * x, axis=-1, keepdims=True) + EPS) * gain


def _mm_body(*refs, has_gain, n_extra, epilogue, stage):
    i = 0
    a_ref = refs[i]; i += 1
    gain_ref = None
    if has_gain:
        gain_ref = refs[i]; i += 1
    w_ref = refs[i]; i += 1
    extra = refs[i:i + n_extra]; i += n_extra
    o_ref = refs[i]; i += 1
    if stage:
        h_ref = refs[i]

        @pl.when(pl.program_id(1) == 0)
        def _():
            a = a_ref[...].astype(f32)
            if has_gain:
                a = _rms(a, gain_ref[...])
            h_ref[...] = a.astype(bf16)

        lhs = h_ref[...]
    else:
        lhs = a_ref[...]
    acc = jnp.dot(lhs, w_ref[...], preferred_element_type=f32)
    o_ref[...] = epilogue(acc, *[e[...] for e in extra]).astype(o_ref.dtype)


def _matmul(a, w, *, tm, tn, out_dtype, gain=None, extras=(), epilogue=None, name="mm"):
    M, K = a.shape
    N = w.shape[1]
    assert M % tm == 0 and N % tn == 0
    stage = gain is not None or a.dtype != bf16
    if epilogue is None:
        epilogue = lambda acc: acc
    in_specs = [pl.BlockSpec((tm, K), lambda i, j: (i, 0))]
    args = [a]
    if gain is not None:
        in_specs.append(pl.BlockSpec((1, K), lambda i, j: (0, 0)))
        args.append(gain.reshape(1, K).astype(f32))
    in_specs.append(pl.BlockSpec((K, tn), lambda i, j: (0, j)))
    args.append(w)
    for arr, col0 in extras:
        assert col0 % tn == 0
        off = col0 // tn
        in_specs.append(pl.BlockSpec((tm, tn), lambda i, j, off=off: (i, j + off)))
        args.append(arr)
    body = functools.partial(_mm_body, has_gain=gain is not None, n_extra=len(extras), epilogue=epilogue,
                             stage=stage)
    return pl.pallas_call(
        body,
        grid=(M // tm, N // tn),
        in_specs=in_specs,
        out_specs=pl.BlockSpec((tm, tn), lambda i, j: (i, j)),
        out_shape=jax.ShapeDtypeStruct((M, N), out_dtype),
        scratch_shapes=[pltpu.VMEM((tm, K), bf16)] if stage else [],
        compiler_params=_cparams(("parallel", "arbitrary")),
        name=name,
    )(*args)


def _rglru_body(xr_ref, gr_ref, cw_ref, cb_ref, wa_ref, ba_ref, wi_ref, bi_ref, sp_ref, o_ref,
                tail_ref, carry_ref, a_ref, u_ref, *, tc):
    c = pl.program_id(1)

    @pl.when(c == 0)
    def _():
        tail_ref[...] = jnp.zeros_like(tail_ref)
        carry_ref[...] = jnp.zeros_like(carry_ref)

    x = xr_ref[...].astype(f32)
    d = x.shape[1]
    row = lax.broadcasted_iota(jnp.int32, (tc, 1), 0)
    row8 = lax.broadcasted_iota(jnp.int32, (SUBLANES, 1), 0)
    tail = tail_ref[...]
    xc = cb_ref[...] + cw_ref[CONV_WIDTH - 1:CONV_WIDTH, :] * x
    for j in range(1, CONV_WIDTH):
        cur = pltpu.roll(x, j, 0)
        halo = pltpu.roll(tail, j, 0)
        head = jnp.where(row8 < j, halo, cur[0:SUBLANES])
        sh = jnp.concatenate([head, cur[SUBLANES:]], axis=0)
        xc = xc + cw_ref[CONV_WIDTH - 1 - j:CONV_WIDTH - j, :] * sh
    tail_ref[...] = x[tc - SUBLANES:]

    xcb = xc.astype(bf16)
    for s in range(d // RNN_SUPER):
        sl = slice(s * RNN_SUPER, (s + 1) * RNN_SUPER)
        za = jnp.dot(xcb[:, sl], wa_ref[s], preferred_element_type=f32) + ba_ref[:, sl]
        zi = jnp.dot(xcb[:, sl], wi_ref[s], preferred_element_type=f32) + bi_ref[:, sl]
        log_a = (-LRU_C) * _sigmoid(za) * sp_ref[:, sl]
        a = jnp.exp(log_a)
        u = jnp.sqrt(-jnp.tanh(log_a) * (a * a + 1.0)) * (_sigmoid(zi) * xc[:, sl])
        r8 = row % SUBLANES
        for k in (1, 2, 4):
            a_s = pltpu.roll(a, k, 0)
            u_s = pltpu.roll(u, k, 0)
            m = r8 >= k
            u = jnp.where(m, a * u_s + u, u)
            a = jnp.where(m, a * a_s, a)
        a_ref[:, sl] = a
        u_ref[:, sl] = u

    def group(g, h_prev):
        r0 = pl.multiple_of(g * SUBLANES, SUBLANES)
        h = a_ref[pl.ds(r0, SUBLANES), :] * h_prev + u_ref[pl.ds(r0, SUBLANES), :]
        u_ref[pl.ds(r0, SUBLANES), :] = h
        return jnp.broadcast_to(h[SUBLANES - 1:SUBLANES, :], h.shape)

    carry_ref[...] = lax.fori_loop(0, tc // SUBLANES, group, carry_ref[...])
    o_ref[...] = (u_ref[...] * _gelu(gr_ref[...].astype(f32))).astype(o_ref.dtype)


def _rglru(proj, B, S, d_rnn, conv_w, conv_b, wa, ba, wi, bi, sp, *, tc):
    nc = S // tc
    ns = d_rnn // RNN_SUPER
    row = lambda b, c: (b * nc + c, 0)
    const2 = lambda b, c: (0, 0)
    const3 = lambda b, c: (0, 0, 0)
    return pl.pallas_call(
        functools.partial(_rglru_body, tc=tc),
        grid=(B, nc),
        in_specs=[
            pl.BlockSpec((tc, d_rnn), row),
            pl.BlockSpec((tc, d_rnn), lambda b, c: (b * nc + c, 1)),
            pl.BlockSpec((CONV_WIDTH, d_rnn), const2),
            pl.BlockSpec((1, d_rnn), const2),
            pl.BlockSpec((ns, RNN_SUPER, RNN_SUPER), const3),
            pl.BlockSpec((1, d_rnn), const2),
            pl.BlockSpec((ns, RNN_SUPER, RNN_SUPER), const3),
            pl.BlockSpec((1, d_rnn), const2),
            pl.BlockSpec((1, d_rnn), const2),
        ],
        out_specs=pl.BlockSpec((tc, d_rnn), row),
        out_shape=jax.ShapeDtypeStruct((B * S, d_rnn), bf16),
        scratch_shapes=[pltpu.VMEM((SUBLANES, d_rnn), f32), pltpu.VMEM((SUBLANES, d_rnn), f32),
                        pltpu.VMEM((tc, d_rnn), f32), pltpu.VMEM((tc, d_rnn), f32)],
        compiler_params=_cparams(("parallel", "arbitrary")),
        name="rglru",
    )(proj, proj, conv_w, conv_b, wa, ba, wi, bi, sp)


def _super_blocks(w):
    nb, bd, _ = w.shape
    per = RNN_SUPER // bd
    out = jnp.zeros((nb // per, RNN_SUPER, RNN_SUPER), w.dtype)
    for k in range(per):
        out = out.at[:, k * bd:(k + 1) * bd, k * bd:(k + 1) * bd].set(w[k::per])
    return out.astype(bf16)


def _compress_body(x_ref, pos_ref, w1_ref, w2_ref, gain_ref, o_ref):
    kind = pl.program_id(1)
    x = x_ref[...].astype(f32)
    nrow = x.shape[0]
    half = x.shape[1]
    first = (x + pos_ref[:, :half]).astype(bf16)
    second = (x + pos_ref[:, half:]).astype(bf16)
    p = jnp.dot(first, w1_ref[:half, :], preferred_element_type=f32)
    q = jnp.dot(second, w1_ref[half:, :], preferred_element_type=f32)
    hid = _gelu(p + pltpu.roll(q, nrow - 1, 0))
    out = jnp.dot(hid.astype(bf16), w2_ref[...], preferred_element_type=f32)
    out = jnp.where(kind == 0, _rms(out, gain_ref[...]), out)
    rowi = lax.broadcasted_iota(jnp.int32, out.shape, 0)
    o_ref[...] = jnp.where(rowi < nrow - 1, out, 0.0).astype(o_ref.dtype)


def _compress(xk, pos, w1, w2, gain):
    B, _, G, nrow, width = xk.shape
    hid = w1.shape[-1]
    return pl.pallas_call(
        _compress_body,
        grid=(B, 2, G),
        in_specs=[
            pl.BlockSpec((None, None, None, nrow, width), lambda b, k, g: (b, k, g, 0, 0)),
            pl.BlockSpec((None, 1, 2 * width), lambda b, k, g: (k, 0, 0)),
            pl.BlockSpec((None, 2 * width, hid), lambda b, k, g: (k, 0, 0)),
            pl.BlockSpec((None, hid, HEAD_DIM), lambda b, k, g: (k, 0, 0)),
            pl.BlockSpec((1, HEAD_DIM), lambda b, k, g: (0, 0)),
        ],
        out_specs=pl.BlockSpec((None, None, None, nrow, HEAD_DIM), lambda b, k, g: (b, k, g, 0, 0)),
        out_shape=jax.ShapeDtypeStruct((B, 2, G, nrow, HEAD_DIM), bf16),
        compiler_params=_cparams(("parallel", "parallel", "parallel")),
        name="compress",
    )(xk, pos, w1, w2, gain)


def _head_gate(gl, col):
    lane = lax.broadcasted_iota(jnp.int32, gl.shape, 1)
    return _sigmoid(jnp.sum(jnp.where(lane == col, gl, 0.0), axis=1, keepdims=True))


def _slope_table():
    sl = 2.0 ** (-8.0 * np.arange(1, N_HEADS + 1) / N_HEADS)
    tab = np.zeros((N_KV, 1, LANES), np.float32)
    tab[:, 0, :HEADS_PER_KV] = sl.reshape(N_KV, HEADS_PER_KV)
    return jnp.asarray(tab)


def _cmp_attn_body(q_ref, kc_ref, vc_ref, qg_ref, gl_ref, ov_ref, sl_ref, oc_ref, sel_ref, *, tq, n_cmp, n_sel):
    g = pl.program_id(1)
    t0 = pl.program_id(2) * tq
    q = q_ref[...].astype(f32)
    kc = kc_ref[...]
    vc = vc_ref[...]
    ncp = kc.shape[0]
    gl = gl_ref[...].astype(f32)
    t = t0 + lax.broadcasted_iota(jnp.int32, (tq, 1), 0)
    cidx = lax.broadcasted_iota(jnp.int32, (1, ncp), 1)
    cend = cidx * CMP_STRIDE + (CMP_BLOCK - 1)
    valid = (cend <= t) & (cidx < n_cmp)
    rel = (cend - t0).astype(f32)
    scale = HEAD_DIM ** -0.5
    imp = jnp.zeros((tq, LANES), f32)
    for h in range(HEADS_PER_KV):
        qh = (_rms(q[:, h * HEAD_DIM:(h + 1) * HEAD_DIM], qg_ref[...]) * scale).astype(bf16)
        s = lax.dot_general(qh, kc, (((1,), (1,)), ((), ())), preferred_element_type=f32)
        s = jnp.where(valid, s + sl_ref[:, h:h + 1] * rel, NEG)
        m = jnp.max(s, axis=1, keepdims=True)
        e = jnp.where(valid, jnp.exp(s - m), 0.0)
        den = jnp.sum(e, axis=1, keepdims=True)
        p = (e / jnp.where(den > 0.0, den, 1.0)).astype(bf16)
        o = jnp.dot(p, vc, preferred_element_type=f32)
        oc_ref[:, h * HEAD_DIM:(h + 1) * HEAD_DIM] = (o * _head_gate(gl, g * HEADS_PER_KV + h)).astype(oc_ref.dtype)
        imp = imp + jnp.dot(p, ov_ref[...], preferred_element_type=f32)

    j = lax.broadcasted_iota(jnp.int32, (1, LANES), 1)
    cur = t // SEL_BLOCK
    forced = (j == 0) | (j == cur) | (j == cur - 1)
    future = j * SEL_BLOCK > t
    imp = jnp.where(forced, FORCE, jnp.where(future, NEG, imp))
    imp = jnp.where(j < n_sel, imp, 2.0 * NEG)
    imp_t = imp.T[0:n_sel]
    row8 = lax.broadcasted_iota(jnp.int32, (SUBLANES, 1), 0)
    groups = [imp_t[r * SUBLANES:(r + 1) * SUBLANES] for r in range(n_sel // SUBLANES)]
    rows = []
    for jj in range(n_sel):
        v = imp_t[jj:jj + 1, :]
        part = jnp.zeros((SUBLANES, tq), f32)
        for r, blk in enumerate(groups):
            lo = r * SUBLANES
            if lo + SUBLANES <= jj:
                ahead = jnp.where(blk >= v, 1.0, 0.0)
            elif lo > jj:
                ahead = jnp.where(blk > v, 1.0, 0.0)
            else:
                ahead = jnp.where(row8 < jj - lo, jnp.where(blk >= v, 1.0, 0.0), jnp.where(blk > v, 1.0, 0.0))
            part = part + ahead
        rows.append(jnp.sum(part, axis=0, keepdims=True))
    rank = jnp.concatenate(rows, axis=0)
    neg_t = jnp.where(rank < float(min(SEL_TOPK, n_sel)), 0.0, NEG)
    pieces = [jnp.zeros((SEL_LANE0, tq), f32), neg_t]
    if SEL_LANE0 + n_sel < LANES:
        pieces.append(jnp.zeros((LANES - SEL_LANE0 - n_sel, tq), f32))
    sel_ref[...] = jnp.concatenate(pieces, axis=0).T.astype(sel_ref.dtype)


def _cmp_attn(proj, gl, kvc, q_gain, overlap, B, S, q_col0, *, tq):
    G = N_KV
    ncp = kvc.shape[3]
    n_cmp = (S - CMP_BLOCK) // CMP_STRIDE + 1
    n_sel = S // SEL_BLOCK
    nq = S // tq
    gw = HEADS_PER_KV * HEAD_DIM
    qb0 = q_col0 // gw
    assert SEL_LANE0 + n_sel <= LANES
    oc, sel = pl.pallas_call(
        functools.partial(_cmp_attn_body, tq=tq, n_cmp=n_cmp, n_sel=n_sel),
        grid=(B, G, nq),
        in_specs=[
            pl.BlockSpec((tq, gw), lambda b, g, i: (b * nq + i, qb0 + g)),
            pl.BlockSpec((None, None, None, ncp, HEAD_DIM), lambda b, g, i: (b, 0, g, 0, 0)),
            pl.BlockSpec((None, None, None, ncp, HEAD_DIM), lambda b, g, i: (b, 1, g, 0, 0)),
            pl.BlockSpec((1, HEAD_DIM), lambda b, g, i: (0, 0)),
            pl.BlockSpec((tq, LANES), lambda b, g, i: (b * nq + i, 0)),
            pl.BlockSpec((ncp, LANES), lambda b, g, i: (0, 0)),
            pl.BlockSpec((None, 1, LANES), lambda b, g, i: (g, 0, 0)),
        ],
        out_specs=[
            pl.BlockSpec((tq, gw), lambda b, g, i: (b * nq + i, g)),
            pl.BlockSpec((None, None, tq, LANES), lambda b, g, i: (b, g, i, 0)),
        ],
        out_shape=[jax.ShapeDtypeStruct((B * S, N_HEADS * HEAD_DIM), bf16),
                   jax.ShapeDtypeStruct((B, G, S, LANES), bf16)],
        compiler_params=_cparams(("parallel", "parallel", "parallel")),
        name="cmp_attn",
    )(proj, kvc, kvc, q_gain, gl, overlap, _slope_table())
    return oc, sel


def _sel_win_body(q_ref, ks_ref, vs_ref, kw_ref, vw_ref, qg_ref, kg_ref, gl_ref, sel_ref, oc_ref, sl_ref, o_ref,
                  ksa_ref, kwa_ref, s_ref, mx_ref, l_ref, acc_ref, *, tq):
    g = pl.program_id(1)
    qi = pl.program_id(2)
    hk = HEADS_PER_KV
    nw = WINDOW // tq
    nl = tq // LANES
    nt = (((1,), (1,)), ((), ()))

    @pl.when(qi == 0)
    def _():
        n_keys = ks_ref.shape[0]
        kpos = lax.broadcasted_iota(jnp.int32, (n_keys, LANES), 0)
        lane = lax.broadcasted_iota(jnp.int32, (n_keys, LANES), 1)
        blk = kpos // SEL_BLOCK
        pos_cols = jnp.where(lane < N_SLOPE_PARTS, blk.astype(f32),
                             jnp.where(lane < 2 * N_SLOPE_PARTS, (kpos % SEL_BLOCK).astype(f32), 0.0))
        onehot = jnp.where(lane - SEL_LANE0 == blk, 1.0, 0.0)
        ksa_ref[:, :HEAD_DIM] = _rms(ks_ref[...].astype(f32), kg_ref[0:1, :]).astype(bf16)
        ksa_ref[:, HEAD_DIM:] = (pos_cols + onehot).astype(bf16)
        kwa_ref[:, :HEAD_DIM] = _rms(kw_ref[...].astype(f32), kg_ref[1:2, :]).astype(bf16)
        kwa_ref[:, HEAD_DIM:] = pos_cols.astype(bf16)

    q = q_ref[...].astype(f32)
    scale = HEAD_DIM ** -0.5
    qx = sel_ref[...].astype(f32)
    qa = jnp.concatenate(
        [jnp.concatenate([(_rms(q[:, h * HEAD_DIM:(h + 1) * HEAD_DIM], qg_ref[...]) * scale).astype(bf16),
                          (qx + sl_ref[h:h + 1, :]).astype(bf16)], axis=1) for h in range(hk)],
        axis=0)
    r = lax.broadcasted_iota(jnp.int32, (tq, tq), 0)
    c = lax.broadcasted_iota(jnp.int32, (tq, tq), 1)
    mask_diag = jnp.concatenate([jnp.where(c <= r, 0.0, NEG)] * hk, axis=0)
    mask_edge = jnp.concatenate([jnp.where(c > r, 0.0, NEG)] * hk, axis=0)

    def reset():
        mx_ref[...] = jnp.full_like(mx_ref, NEG)
        l_ref[...] = jnp.zeros_like(l_ref)
        acc_ref[...] = jnp.zeros_like(acc_ref)

    def score_tile(ka_ref, kt, slot, mask):
        k0 = pl.multiple_of(kt * tq, tq)
        s = lax.dot_general(qa, ka_ref[pl.ds(k0, tq), :], nt, preferred_element_type=f32)
        if mask is not None:
            s = s + mask
        s_ref[slot] = s
        part = s[:, 0:LANES]
        for i in range(1, nl):
            part = jnp.maximum(part, s[:, i * LANES:(i + 1) * LANES])
        mx_ref[...] = jnp.maximum(mx_ref[...], part)

    def value_tile(v_ref, kt, slot, m):
        k0 = pl.multiple_of(kt * tq, tq)
        p = jnp.exp(s_ref[slot] - m)
        part = p[:, 0:LANES]
        for i in range(1, nl):
            part = part + p[:, i * LANES:(i + 1) * LANES]
        l_ref[...] += part
        acc_ref[...] += jnp.dot(p.astype(bf16), v_ref[pl.ds(k0, tq), :], preferred_element_type=f32)

    def finish():
        return acc_ref[...] / jnp.sum(l_ref[...], axis=1, keepdims=True)

    reset()

    def sel_scores(kt, carry):
        score_tile(ksa_ref, kt, kt, None)
        return carry

    lax.fori_loop(0, qi, sel_scores, 0)
    score_tile(ksa_ref, qi, qi, mask_diag)
    m_sel = jnp.max(mx_ref[...], axis=1, keepdims=True)

    def sel_values(kt, carry):
        value_tile(vs_ref, kt, kt, m_sel)
        return carry

    lax.fori_loop(0, qi + 1, sel_values, 0)
    o_sel = finish()

    reset()
    for d in range(nw, 0, -1):
        mask = mask_edge if d == nw else None
        pl.when(qi >= d)(functools.partial(score_tile, kwa_ref, qi - d, nw - d, mask))
    score_tile(kwa_ref, qi, nw, mask_diag)
    m_win = jnp.max(mx_ref[...], axis=1, keepdims=True)
    for d in range(nw, 0, -1):
        pl.when(qi >= d)(functools.partial(value_tile, vw_ref, qi - d, nw - d, m_win))
    value_tile(vw_ref, qi, nw, m_win)
    o_win = finish()

    gl = gl_ref[...].astype(f32)
    for h in range(hk):
        col = g * hk + h
        rows = slice(h * tq, (h + 1) * tq)
        o = (oc_ref[:, h * HEAD_DIM:(h + 1) * HEAD_DIM].astype(f32)
             + _head_gate(gl, N_HEADS + col) * o_sel[rows] + _head_gate(gl, 2 * N_HEADS + col) * o_win[rows])
        o_ref[:, h * HEAD_DIM:(h + 1) * HEAD_DIM] = o.astype(o_ref.dtype)


def _alibi_parts_table():
    sl = (2.0 ** (-8.0 * np.arange(1, N_HEADS + 1) / N_HEADS)).astype(np.float32)
    tab = np.zeros((N_HEADS, LANES), np.float32)
    rest = sl
    for i in range(N_SLOPE_PARTS):
        piece = rest.astype(bf16).astype(np.float32)
        rest = rest - piece
        tab[:, i] = piece * SEL_BLOCK
        tab[:, N_SLOPE_PARTS + i] = piece
    return jnp.asarray(tab.reshape(N_KV, HEADS_PER_KV, LANES))


def _sel_win_attn(proj, gl, sel, oc, q_gain, k_gains2, B, S, q_col0, kv_col0, *, tq):
    G = N_KV
    nq = S // tq
    gw = HEADS_PER_KV * HEAD_DIM
    qb0 = q_col0 // gw
    kb0 = kv_col0 // HEAD_DIM
    assert WINDOW % tq == 0 and tq % LANES == 0 and 2 * N_SLOPE_PARTS <= SEL_LANE0
    assert S // SEL_BLOCK <= 256
    n_slots = max(nq, WINDOW // tq + 1)

    def kv_spec(kind):
        return pl.BlockSpec((S, HEAD_DIM), lambda b, g, i, kind=kind: (b, kb0 + kind * G + g))

    return pl.pallas_call(
        functools.partial(_sel_win_body, tq=tq),
        grid=(B, G, nq),
        in_specs=[
            pl.BlockSpec((tq, gw), lambda b, g, i: (b * nq + i, qb0 + g)),
            kv_spec(2), kv_spec(3), kv_spec(4), kv_spec(5),
            pl.BlockSpec((1, HEAD_DIM), lambda b, g, i: (0, 0)),
            pl.BlockSpec((2, HEAD_DIM), lambda b, g, i: (0, 0)),
            pl.BlockSpec((tq, LANES), lambda b, g, i: (b * nq + i, 0)),
            pl.BlockSpec((None, None, tq, LANES), lambda b, g, i: (b, g, i, 0)),
            pl.BlockSpec((tq, gw), lambda b, g, i: (b * nq + i, g)),
            pl.BlockSpec((None, HEADS_PER_KV, LANES), lambda b, g, i: (g, 0, 0)),
        ],
        out_specs=pl.BlockSpec((tq, gw), lambda b, g, i: (b * nq + i, g)),
        out_shape=jax.ShapeDtypeStruct((B * S, N_HEADS * HEAD_DIM), bf16),
        scratch_shapes=[pltpu.VMEM((S, 2 * HEAD_DIM), bf16), pltpu.VMEM((S, 2 * HEAD_DIM), bf16),
                        pltpu.VMEM((n_slots, HEADS_PER_KV * tq, tq), f32),
                        pltpu.VMEM((HEADS_PER_KV * tq, LANES), f32), pltpu.VMEM((HEADS_PER_KV * tq, LANES), f32),
                        pltpu.VMEM((HEADS_PER_KV * tq, HEAD_DIM), f32)],
        compiler_params=_cparams(("parallel", "parallel", "arbitrary")),
        name="sel_win_attn",
    )(proj, proj, proj, proj, proj, q_gain, k_gains2, gl, sel, oc, _alibi_parts_table())


def _router_body(x_ref, gain_ref, wr_ref, br_ref, xn_ref, info_ref, cnt_ref, carry_ref, *, tm, n_exp):
    @pl.when(pl.program_id(0) == 0)
    def _():
        carry_ref[...] = jnp.zeros_like(carry_ref)

    xn = _rms(x_ref[...], gain_ref[...])
    xn_ref[...] = xn
    lane = lax.broadcasted_iota(jnp.int32, (tm, LANES), 1)
    x_hi = xn.astype(bf16)
    x_lo = (xn - x_hi.astype(f32)).astype(bf16)
    logits = (jnp.dot(x_hi, wr_ref[0], preferred_element_type=f32)
              + jnp.dot(x_lo, wr_ref[0], preferred_element_type=f32)
              + jnp.dot(x_hi, wr_ref[1], preferred_element_type=f32)) + br_ref[...]
    logits = jnp.where(lane < n_exp, logits, 2.0 * NEG)
    hots, vals = [], []
    for _ in range(TOP_K):
        mx = jnp.max(logits, axis=1, keepdims=True)
        idx = jnp.min(jnp.where(logits == mx, lane, LANES), axis=1, keepdims=True)
        hot = lane == idx
        hots.append(hot)
        vals.append(mx)
        logits = jnp.where(hot, 2.0 * NEG, logits)
    es = [jnp.exp(v - vals[0]) for v in vals]
    den = es[0] + es[1] + es[2] + es[3]
    onehot = jnp.zeros((tm, LANES), f32)
    for hot in hots:
        onehot = onehot + hot.astype(f32)
    r = lax.broadcasted_iota(jnp.int32, (tm, tm), 0)
    c = lax.broadcasted_iota(jnp.int32, (tm, tm), 1)
    before = jnp.dot((c < r).astype(bf16), onehot.astype(bf16), preferred_element_type=f32)
    rank = carry_ref[...] + before
    carry_ref[...] = carry_ref[...] + jnp.sum(onehot, axis=0, keepdims=True)
    cnt_ref[...] = carry_ref[...]
    info = jnp.zeros((tm, LANES), f32)
    for k in range(TOP_K):
        e_k = jnp.sum(jnp.where(hots[k], lane, 0), axis=1, keepdims=True).astype(f32)
        r_k = jnp.sum(jnp.where(hots[k], rank, 0.0), axis=1, keepdims=True)
        info = jnp.where(lane == k, e_k, info)
        info = jnp.where(lane == TOP_K + k, es[k] / den, info)
        info = jnp.where(lane == 2 * TOP_K + k, r_k, info)
    info_ref[...] = info


def _router(x, gain, wr, br, *, tm):
    T, D = x.shape
    return pl.pallas_call(
        functools.partial(_router_body, tm=tm, n_exp=N_EXPERTS),
        grid=(T // tm,),
        in_specs=[
            pl.BlockSpec((tm, D), lambda i: (i, 0)),
            pl.BlockSpec((1, D), lambda i: (0, 0)),
            pl.BlockSpec((2, D, LANES), lambda i: (0, 0, 0)),
            pl.BlockSpec((1, LANES), lambda i: (0, 0)),
        ],
        out_specs=[
            pl.BlockSpec((tm, D), lambda i: (i, 0)),
            pl.BlockSpec((tm, LANES), lambda i: (i, 0)),
            pl.BlockSpec((1, LANES), lambda i: (0, 0)),
        ],
        out_shape=[jax.ShapeDtypeStruct((T, D), f32), jax.ShapeDtypeStruct((T, LANES), f32),
                   jax.ShapeDtypeStruct((1, LANES), f32)],
        scratch_shapes=[pltpu.VMEM((1, LANES), f32)],
        compiler_params=_cparams(("arbitrary",)),
        name="router",
    )(x, gain, wr, br)


def _dispatch_body(pos_ref, xn_ref, xs_in_ref, xs_ref, sem, *, tm):
    del xs_in_ref

    def row_copy(t, k):
        return pltpu.make_async_copy(xn_ref.at[pl.ds(t, 1)], xs_ref.at[pl.ds(pos_ref[t * TOP_K + k], 1)], sem)

    def issue(t, c):
        for k in range(TOP_K):
            row_copy(t, k).start()
        return c

    def drain(t, c):
        for k in range(TOP_K):
            row_copy(t, k).wait()
        return c

    lax.fori_loop(0, tm, issue, 0)
    lax.fori_loop(0, tm, drain, 0)


def _dispatch(pos_flat, xn, n_rows, *, tm):
    T, D = xn.shape
    xs0 = jnp.zeros((n_rows, D), xn.dtype)
    return pl.pallas_call(
        functools.partial(_dispatch_body, tm=tm),
        grid=(T // tm,),
        in_specs=[
            pl.BlockSpec((tm * TOP_K,), lambda i: (i,), memory_space=pltpu.SMEM),
            pl.BlockSpec((tm, D), lambda i: (i, 0)),
            pl.BlockSpec(memory_space=pl.ANY),
        ],
        out_specs=pl.BlockSpec(memory_space=pl.ANY),
        out_shape=jax.ShapeDtypeStruct((n_rows, D), xn.dtype),
        scratch_shapes=[pltpu.SemaphoreType.DMA],
        input_output_aliases={2: 0},
        compiler_params=_cparams(("arbitrary",), disable_bounds_checks=True, has_side_effects=True),
        name="moe_dispatch",
    )(pos_flat, xn, xs0)


def _new_expert(be_ref, i):
    return (i == 0) | (be_ref[i] != be_ref[jnp.maximum(i - 1, 0)])


def _up_body(be_ref, nu_ref, x_ref, wg_ref, wl_ref, bg_ref, bl_ref, h_ref, wgs_ref, wls_ref):
    i = pl.program_id(1)

    @pl.when(_new_expert(be_ref, i) & (i < nu_ref[0]))
    def _():
        wgs_ref[...] = wg_ref[...].astype(bf16)
        wls_ref[...] = wl_ref[...].astype(bf16)

    @pl.when(i < nu_ref[0])
    def _():
        x = x_ref[...].astype(bf16)
        glu = jnp.minimum(jnp.dot(x, wgs_ref[...], preferred_element_type=f32) + bg_ref[...], SWIGLU_LIMIT)
        lin = jnp.clip(jnp.dot(x, wls_ref[...], preferred_element_type=f32) + bl_ref[...], -SWIGLU_LIMIT,
                       SWIGLU_LIMIT)
        h_ref[...] = (glu * _sigmoid(SWIGLU_ALPHA * glu) * (lin + 1.0)).astype(h_ref.dtype)

    @pl.when(i >= nu_ref[0])
    def _():
        h_ref[...] = jnp.zeros_like(h_ref)


def _moe_up(block_e, n_used, xs, w_up, b_up, *, rb, tf):
    R, D = xs.shape
    d_ff = w_up.shape[2] // 2
    nj = d_ff // tf
    nb = R // rb
    rowmap = lambda j, i, be, nu: (jnp.minimum(i, nu[0] - 1), 0)
    return pl.pallas_call(
        _up_body,
        grid_spec=pltpu.PrefetchScalarGridSpec(
            num_scalar_prefetch=2,
            grid=(nj, nb),
            in_specs=[
                pl.BlockSpec((rb, D), rowmap),
                pl.BlockSpec((None, D, tf), lambda j, i, be, nu: (be[i], 0, j)),
                pl.BlockSpec((None, D, tf), lambda j, i, be, nu: (be[i], 0, nj + j)),
                pl.BlockSpec((None, 1, tf), lambda j, i, be, nu: (be[i], 0, j)),
                pl.BlockSpec((None, 1, tf), lambda j, i, be, nu: (be[i], 0, nj + j)),
            ],
            out_specs=pl.BlockSpec((rb, tf), lambda j, i, be, nu: (i, j)),
            scratch_shapes=[pltpu.VMEM((D, tf), bf16), pltpu.VMEM((D, tf), bf16)],
        ),
        out_shape=jax.ShapeDtypeStruct((R, d_ff), bf16),
        compiler_params=pltpu.CompilerParams(dimension_semantics=("arbitrary", "arbitrary"),
                                             vmem_limit_bytes=MOE_VMEM_LIMIT),
        name="moe_up",
    )(block_e, n_used, xs, w_up, w_up, b_up, b_up)


def _down_body(be_ref, nu_ref, h_ref, w_ref, b_ref, y_ref, ws_ref):
    i = pl.program_id(1)

    @pl.when(_new_expert(be_ref, i) & (i < nu_ref[0]))
    def _():
        ws_ref[...] = w_ref[...].astype(bf16)

    @pl.when(i < nu_ref[0])
    def _():
        y_ref[...] = jnp.dot(h_ref[...], ws_ref[...], preferred_element_type=f32) + b_ref[...]

    @pl.when(i >= nu_ref[0])
    def _():
        y_ref[...] = jnp.zeros_like(y_ref)


def _moe_down(block_e, n_used, h, w_down, b_down, *, rb, tn):
    R, F = h.shape
    D = w_down.shape[2]
    return pl.pallas_call(
        _down_body,
        grid_spec=pltpu.PrefetchScalarGridSpec(
            num_scalar_prefetch=2,
            grid=(D // tn, R // rb),
            in_specs=[
                pl.BlockSpec((rb, F), lambda j, i, be, nu: (jnp.minimum(i, nu[0] - 1), 0)),
                pl.BlockSpec((None, F, tn), lambda j, i, be, nu: (be[i], 0, j)),
                pl.BlockSpec((None, 1, tn), lambda j, i, be, nu: (be[i], 0, j)),
            ],
            out_specs=pl.BlockSpec((rb, tn), lambda j, i, be, nu: (i, j)),
            scratch_shapes=[pltpu.VMEM((F, tn), bf16)],
        ),
        out_shape=jax.ShapeDtypeStruct((R, D), f32),
        compiler_params=pltpu.CompilerParams(dimension_semantics=("arbitrary", "arbitrary"),
                                             vmem_limit_bytes=MOE_VMEM_LIMIT),
        name="moe_down",
    )(block_e, n_used, h, w_down, b_down)


def _combine_body(pos_ref, info_ref, x_ref, ys_ref, o_ref, buf_ref, sem, *, tm):
    def row_copy(t, k):
        return pltpu.make_async_copy(ys_ref.at[pl.ds(pos_ref[t * TOP_K + k], 1)], buf_ref.at[k, pl.ds(t, 1)], sem)

    def issue(t, c):
        for k in range(TOP_K):
            row_copy(t, k).start()
        return c

    def drain(t, c):
        for k in range(TOP_K):
            row_copy(t, k).wait()
        return c

    lax.fori_loop(0, tm, issue, 0)
    lax.fori_loop(0, tm, drain, 0)
    info = info_ref[...]
    lane = lax.broadcasted_iota(jnp.int32, info.shape, 1)
    y = x_ref[...]
    for k in range(TOP_K):
        gate = jnp.sum(jnp.where(lane == TOP_K + k, info, 0.0), axis=1, keepdims=True)
        y = y + gate * buf_ref[k]
    o_ref[...] = y


def _combine(pos_flat, info, x, ys, *, tm):
    T, D = x.shape
    return pl.pallas_call(
        functools.partial(_combine_body, tm=tm),
        grid=(T // tm,),
        in_specs=[
            pl.BlockSpec((tm * TOP_K,), lambda i: (i,), memory_space=pltpu.SMEM),
            pl.BlockSpec((tm, LANES), lambda i: (i, 0)),
            pl.BlockSpec((tm, D), lambda i: (i, 0)),
            pl.BlockSpec(memory_space=pl.ANY),
        ],
        out_specs=pl.BlockSpec((tm, D), lambda i: (i, 0)),
        out_shape=jax.ShapeDtypeStruct((T, D), f32),
        scratch_shapes=[pltpu.VMEM((TOP_K, tm, D), f32), pltpu.SemaphoreType.DMA],
        compiler_params=_cparams(("arbitrary",), disable_bounds_checks=True),
        name="moe_combine",
    )(pos_flat, info, x, ys)


def _pick(n, *cands):
    for c in cands:
        if n % c == 0:
            return c
    return n


def _layer(x2d, p2d, B, S, norm_mix, w_in, conv_w, conv_b, w_rg_a, b_rg_a, w_rg_i, b_rg_i, lru_lambda, w_rnn_out,
           q_gain, k_gains, cmp_pos_k, cmp_w1_k, cmp_w2_k, cmp_pos_v, cmp_w1_v, cmp_w2_v, w_nsa_out, w_out,
           norm_moe, w_router, b_router, w_up, b_up, w_down, b_down, w_ple_proj, ple_norm, w_ple_gate):
    T, D = x2d.shape
    d_rnn = conv_w.shape[1]
    q_w = N_HEADS * HEAD_DIM
    kv_w = 6 * N_KV * HEAD_DIM
    n_gate = 3 * N_HEADS
    q0 = 2 * d_rnn
    kv0 = q0 + q_w
    gn0 = kv0 + kv_w
    gm0 = gn0 + n_gate
    tm = _pick(T, 1024, 512, 256)

    w_main = jnp.concatenate([w_in[:, :gn0], w_in[:, gm0:]], axis=1).astype(bf16)
    w_gate = jnp.pad(w_in[:, gn0:gm0], ((0, 0), (0, LANES - n_gate))).astype(bf16)
    proj = _matmul(x2d, w_main, tm=tm, tn=1024, out_dtype=bf16, gain=norm_mix, name="in_proj")
    gl = _matmul(x2d, w_gate, tm=tm, tn=LANES, out_dtype=f32, gain=norm_mix, name="in_proj_gates")
    merge0 = gn0

    sp = jax.nn.softplus(-lru_lambda.astype(f32)).reshape(1, d_rnn)
    hg = _rglru(proj, B, S, d_rnn, conv_w, conv_b.reshape(1, d_rnn), _super_blocks(w_rg_a),
                b_rg_a.reshape(1, d_rnn), _super_blocks(w_rg_i), b_rg_i.reshape(1, d_rnn), sp,
                tc=_pick(S, 256, 128))

    nrow = S // CMP_STRIDE
    kv_cmp = proj[:, kv0:kv0 + 2 * N_KV * HEAD_DIM].reshape(B, nrow, CMP_STRIDE, 2, N_KV, HEAD_DIM)
    kv_cmp = kv_cmp.transpose(0, 3, 4, 1, 2, 5).reshape(B, 2, N_KV, nrow, CMP_STRIDE * HEAD_DIM)
    pos = jnp.stack([cmp_pos_k, cmp_pos_v]).reshape(2, 1, CMP_BLOCK * HEAD_DIM).astype(f32)
    w1 = jnp.stack([cmp_w1_k, cmp_w1_v]).astype(bf16)
    w2 = jnp.stack([cmp_w2_k, cmp_w2_v]).astype(bf16)
    kvc = _compress(kv_cmp, pos, w1, w2, k_gains[0:1].astype(f32))

    n_cmp = (S - CMP_BLOCK) // CMP_STRIDE + 1
    n_sel = S // SEL_BLOCK
    c0 = np.arange(nrow)[:, None] * CMP_STRIDE
    s0 = np.arange(LANES)[None, :] * SEL_BLOCK
    overlap = np.clip(np.minimum(c0 + CMP_BLOCK, s0 + SEL_BLOCK) - np.maximum(c0, s0), 0, None) / CMP_STRIDE
    overlap = overlap * (np.arange(nrow)[:, None] < n_cmp) * (np.arange(LANES)[None, :] < n_sel)
    qg = q_gain.reshape(1, HEAD_DIM).astype(f32)
    oc, sel = _cmp_attn(proj, gl, kvc, qg, jnp.asarray(overlap, bf16), B, S, q0, tq=_pick(S, 256, 128))
    o_nsa = _sel_win_attn(proj, gl, sel, oc, qg, k_gains[1:3].astype(f32), B, S, q0, kv0,
                          tq=_pick(S, 256, 128))

    y_nsa = _matmul(o_nsa, w_nsa_out.astype(bf16), tm=tm, tn=1024, out_dtype=bf16,
                    extras=[(proj, merge0 + D)],
                    epilogue=lambda acc, m: _sigmoid(m.astype(f32)) * acc, name="nsa_out")
    merged = _matmul(hg, w_rnn_out.astype(bf16), tm=tm, tn=1024, out_dtype=bf16,
                     extras=[(proj, merge0), (y_nsa, 0)],
                     epilogue=lambda acc, m, y: _sigmoid(m.astype(f32)) * acc + y.astype(f32), name="rnn_out")
    x1 = _matmul(merged, w_out.astype(bf16), tm=tm, tn=1024, out_dtype=f32, extras=[(x2d, 0)],
                 epilogue=lambda acc, xr: xr + acc, name="mix_out")

    x2 = _moe(x1, norm_moe, w_router, b_router, w_up, b_up, w_down, b_down)

    ple_gain = ple_norm.reshape(1, D).astype(f32)
    ple = _ple_proj(p2d, w_ple_proj.astype(bf16), ple_gain, tm=_pick(T, 512, 256))
    return _matmul(x2, w_ple_gate.astype(bf16), tm=tm, tn=1024, out_dtype=f32, extras=[(x2, 0), (ple, 0)],
                   epilogue=lambda acc, xr, pe: xr + _sigmoid(acc) * pe.astype(f32), name="ple_gate")


def _moe(x1, norm_moe, w_router, b_router, w_up, b_up, w_down, b_down):
    T, D = x1.shape
    wr_f = jnp.pad(w_router.astype(f32), ((0, 0), (0, LANES - N_EXPERTS)))
    wr_hi = wr_f.astype(bf16)
    wr = jnp.stack([wr_hi, (wr_f - wr_hi.astype(f32)).astype(bf16)])
    br = jnp.pad(b_router, (0, LANES - N_EXPERTS)).reshape(1, LANES).astype(f32)
    xn, info, cnt = _router(x1, norm_moe.reshape(1, D).astype(f32), wr, br, tm=_pick(T, 512, 256))
    rb = MOE_ROWS
    eid = info[:, 0:TOP_K].astype(jnp.int32)
    rank = info[:, 2 * TOP_K:3 * TOP_K].astype(jnp.int32)
    counts = cnt[0, :N_EXPERTS].astype(jnp.int32)
    padded = (counts + rb - 1) // rb * rb
    pend = jnp.cumsum(padded)
    pstart = pend - padded
    pos_flat = (pstart[eid] + rank).reshape(T * TOP_K)
    n_blocks = (T * TOP_K) // rb + N_EXPERTS
    block_e = jnp.minimum(jnp.sum(jnp.arange(n_blocks)[:, None] * rb >= pend[None, :], axis=1),
                          N_EXPERTS - 1).astype(jnp.int32)
    n_used = (pend[-1:] // rb).astype(jnp.int32)
    xs = _dispatch(pos_flat, xn, n_blocks * rb, tm=_pick(T, 256))
    hmid = _moe_up(block_e, n_used, xs, w_up, b_up.reshape(N_EXPERTS, 1, -1).astype(f32), rb=rb, tf=1024)
    ys = _moe_down(block_e, n_used, hmid, w_down, b_down.reshape(N_EXPERTS, 1, -1).astype(f32), rb=rb,
                   tn=w_down.shape[2])
    return _combine(pos_flat, info, x1, ys, tm=_pick(T, 256))


def _ple_body(p_ref, w_ref, g_ref, o_ref):
    acc = jnp.dot(p_ref[...].astype(bf16), w_ref[...], preferred_element_type=f32)
    o_ref[...] = _rms(acc, g_ref[...]).astype(o_ref.dtype)


def _ple_proj(p2d, w, gain, *, tm):
    T, K = p2d.shape
    D = w.shape[1]
    return pl.pallas_call(
        _ple_body,
        grid=(T // tm,),
        in_specs=[pl.BlockSpec((tm, K), lambda i: (i, 0)), pl.BlockSpec((K, D), lambda i: (0, 0)),
                  pl.BlockSpec((1, D), lambda i: (0, 0))],
        out_specs=pl.BlockSpec((tm, D), lambda i: (i, 0)),
        out_shape=jax.ShapeDtypeStruct((T, D), bf16),
        compiler_params=_cparams(("parallel",)),
        name="ple_proj",
    )(p2d, w, gain)


def kernel(x, p, norm_mix, w_in, conv_w, conv_b, w_rg_a, b_rg_a, w_rg_i, b_rg_i, lru_lambda, w_rnn_out, q_gain, k_gains, cmp_pos_k, cmp_w1_k, cmp_w2_k, cmp_pos_v, cmp_w1_v, cmp_w2_v, w_nsa_out, w_out, norm_moe, w_router, b_router, w_up, b_up, w_down, b_down, w_ple_proj, ple_norm, w_ple_gate):
    B, S, D = x.shape
    depth = p.shape[0]
    x2d = x.reshape(B * S, D)
    for i in range(depth):
        x2d = _layer(x2d, p[i].reshape(B * S, -1), B, S, norm_mix[i], w_in[i], conv_w[i], conv_b[i], w_rg_a[i],
                     b_rg_a[i], w_rg_i[i], b_rg_i[i], lru_lambda[i], w_rnn_out[i], q_gain[i], k_gains[i],
                     cmp_pos_k[i], cmp_w1_k[i], cmp_w2_k[i], cmp_pos_v[i], cmp_w1_v[i], cmp_w2_v[i], w_nsa_out[i],
                     w_out[i], norm_moe[i], w_router[i], b_router[i], w_up[i], b_up[i], w_down[i], b_down[i],
                     w_ple_proj[i], ple_norm[i], w_ple_gate[i])
    return x2d.reshape(B, S, D)
```

```python
import functools
import math

import numpy as np
import jax
import jax.numpy as jnp
from jax import lax
from jax.experimental import pallas as pl
from jax.experimental.pallas import tpu as pltpu

RNN_BLOCKS = 16
CONV_WIDTH = 4
LRU_C = 8.0
N_HEADS = 16
HEAD_DIM = 128
N_KV = 4
HEADS_PER_KV = N_HEADS // N_KV
CMP_BLOCK = 32
CMP_STRIDE = 16
SEL_BLOCK = 64
SEL_TOPK = 16
WINDOW = 512
N_EXPERTS = 32
TOP_K = 4
SWIGLU_LIMIT = 7.0
SWIGLU_ALPHA = 1.702
EPS = 1e-6
NEG = -1e30
FORCE = 1e9

LANES = 128
SUBLANES = 8
RNN_SUPER = 640
VMEM_LIMIT = 56 * 1024 * 1024
MOE_VMEM_LIMIT = 60 * 1024 * 1024
MOE_ROWS = 512
SEL_LANE0 = 64
N_SLOPE_PARTS = 3

bf16 = jnp.bfloat16
f32 = jnp.float32


def _cparams(sem, **kw):
    return pltpu.CompilerParams(dimension_semantics=sem, vmem_limit_bytes=VMEM_LIMIT, **kw)


def _gelu(x):
    return 0.5 * x * (1.0 + jnp.tanh(math.sqrt(2.0 / math.pi) * (x + 0.044715 * (x * x * x))))


def _sigmoid(x):
    return 0.5 * jnp.tanh(0.5 * x) + 0.5


def _rms(x, gain):
    return x * lax.rsqrt(jnp.mean(x * x, axis=-1, keepdims=True) + EPS) * gain


def _mm_body(*refs, has_gain, n_extra, epilogue, stage):
    i = 0
    a_ref = refs[i]; i += 1
    gain_ref = None
    if has_gain:
        gain_ref = refs[i]; i += 1
    w_ref = refs[i]; i += 1
    extra = refs[i:i + n_extra]; i += n_extra
    o_ref = refs[i]; i += 1
    if stage:
        h_ref = refs[i]

        @pl.when(pl.program_id(1) == 0)
        def _():
            a = a_ref[...].astype(f32)
            if has_gain:
                a = _rms(a, gain_ref[...])
            h_ref[...] = a.astype(bf16)

        lhs = h_ref[...]
    else:
        lhs = a_ref[...]
    acc = jnp.dot(lhs, w_ref[...], preferred_element_type=f32)
    o_ref[...] = epilogue(acc, *[e[...] for e in extra]).astype(o_ref.dtype)


def _matmul(a, w, *, tm, tn, out_dtype, gain=None, extras=(), epilogue=None, name="mm"):
    M, K = a.shape
    N = w.shape[1]
    assert M % tm == 0 and N % tn == 0
    stage = gain is not None or a.dtype != bf16
    if epilogue is None:
        epilogue = lambda acc: acc
    in_specs = [pl.BlockSpec((tm, K), lambda i, j: (i, 0))]
    args = [a]
    if gain is not None:
        in_specs.append(pl.BlockSpec((1, K), lambda i, j: (0, 0)))
        args.append(gain.reshape(1, K).astype(f32))
    in_specs.append(pl.BlockSpec((K, tn), lambda i, j: (0, j)))
    args.append(w)
    for arr, col0 in extras:
        assert col0 % tn == 0
        off = col0 // tn
        in_specs.append(pl.BlockSpec((tm, tn), lambda i, j, off=off: (i, j + off)))
        args.append(arr)
    body = functools.partial(_mm_body, has_gain=gain is not None, n_extra=len(extras), epilogue=epilogue,
                             stage=stage)
    return pl.pallas_call(
        body,
        grid=(M // tm, N // tn),
        in_specs=in_specs,
        out_specs=pl.BlockSpec((tm, tn), lambda i, j: (i, j)),
        out_shape=jax.ShapeDtypeStruct((M, N), out_dtype),
        scratch_shapes=[pltpu.VMEM((tm, K), bf16)] if stage else [],
        compiler_params=_cparams(("parallel", "arbitrary")),
        name=name,
    )(*args)


def _rglru_body(xr_ref, gr_ref, cw_ref, cb_ref, wa_ref, ba_ref, wi_ref, bi_ref, sp_ref, o_ref,
                tail_ref, carry_ref, a_ref, u_ref, *, tc):
    c = pl.program_id(1)

    @pl.when(c == 0)
    def _():
        tail_ref[...] = jnp.zeros_like(tail_ref)
        carry_ref[...] = jnp.zeros_like(carry_ref)

    x = xr_ref[...].astype(f32)
    d = x.shape[1]
    row = lax.broadcasted_iota(jnp.int32, (tc, 1), 0)
    row8 = lax.broadcasted_iota(jnp.int32, (SUBLANES, 1), 0)
    tail = tail_ref[...]
    xc = cb_ref[...] + cw_ref[CONV_WIDTH - 1:CONV_WIDTH, :] * x
    for j in range(1, CONV_WIDTH):
        cur = pltpu.roll(x, j, 0)
        halo = pltpu.roll(tail, j, 0)
        head = jnp.where(row8 < j, halo, cur[0:SUBLANES])
        sh = jnp.concatenate([head, cur[SUBLANES:]], axis=0)
        xc = xc + cw_ref[CONV_WIDTH - 1 - j:CONV_WIDTH - j, :] * sh
    tail_ref[...] = x[tc - SUBLANES:]

    xcb = xc.astype(bf16)
    for s in range(d // RNN_SUPER):
        sl = slice(s * RNN_SUPER, (s + 1) * RNN_SUPER)
        za = jnp.dot(xcb[:, sl], wa_ref[s], preferred_element_type=f32) + ba_ref[:, sl]
        zi = jnp.dot(xcb[:, sl], wi_ref[s], preferred_element_type=f32) + bi_ref[:, sl]
        log_a = (-LRU_C) * _sigmoid(za) * sp_ref[:, sl]
        a = jnp.exp(log_a)
        u = jnp.sqrt(-jnp.tanh(log_a) * (a * a + 1.0)) * (_sigmoid(zi) * xc[:, sl])
        r8 = row % SUBLANES
        for k in (1, 2, 4):
            a_s = pltpu.roll(a, k, 0)
            u_s = pltpu.roll(u, k, 0)
            m = r8 >= k
            u = jnp.where(m, a * u_s + u, u)
            a = jnp.where(m, a * a_s, a)
        a_ref[:, sl] = a
        u_ref[:, sl] = u

    def group(g, h_prev):
        r0 = pl.multiple_of(g * SUBLANES, SUBLANES)
        h = a_ref[pl.ds(r0, SUBLANES), :] * h_prev + u_ref[pl.ds(r0, SUBLANES), :]
        u_ref[pl.ds(r0, SUBLANES), :] = h
        return jnp.broadcast_to(h[SUBLANES - 1:SUBLANES, :], h.shape)

    carry_ref[...] = lax.fori_loop(0, tc // SUBLANES, group, carry_ref[...])
    o_ref[...] = (u_ref[...] * _gelu(gr_ref[...].astype(f32))).astype(o_ref.dtype)


def _rglru(proj, B, S, d_rnn, conv_w, conv_b, wa, ba, wi, bi, sp, *, tc):
    nc = S // tc
    ns = d_rnn // RNN_SUPER
    row = lambda b, c: (b * nc + c, 0)
    const2 = lambda b, c: (0, 0)
    const3 = lambda b, c: (0, 0, 0)
    return pl.pallas_call(
        functools.partial(_rglru_body, tc=tc),
        grid=(B, nc),
        in_specs=[
            pl.BlockSpec((tc, d_rnn), row),
            pl.BlockSpec((tc, d_rnn), lambda b, c: (b * nc + c, 1)),
            pl.BlockSpec((CONV_WIDTH, d_rnn), const2),
            pl.BlockSpec((1, d_rnn), const2),
            pl.BlockSpec((ns, RNN_SUPER, RNN_SUPER), const3),
            pl.BlockSpec((1, d_rnn), const2),
            pl.BlockSpec((ns, RNN_SUPER, RNN_SUPER), const3),
            pl.BlockSpec((1, d_rnn), const2),
            pl.BlockSpec((1, d_rnn), const2),
        ],
        out_specs=pl.BlockSpec((tc, d_rnn), row),
        out_shape=jax.ShapeDtypeStruct((B * S, d_rnn), bf16),
        scratch_shapes=[pltpu.VMEM((SUBLANES, d_rnn), f32), pltpu.VMEM((SUBLANES, d_rnn), f32),
                        pltpu.VMEM((tc, d_rnn), f32), pltpu.VMEM((tc, d_rnn), f32)],
        compiler_params=_cparams(("parallel", "arbitrary")),
        name="rglru",
    )(proj, proj, conv_w, conv_b, wa, ba, wi, bi, sp)


def _super_blocks(w):
    nb, bd, _ = w.shape
    per = RNN_SUPER // bd
    out = jnp.zeros((nb // per, RNN_SUPER, RNN_SUPER), w.dtype)
    for k in range(per):
        out = out.at[:, k * bd:(k + 1) * bd, k * bd:(k + 1) * bd].set(w[k::per])
    return out.astype(bf16)


def _compress_body(x_ref, pos_ref, w1_ref, w2_ref, gain_ref, o_ref):
    kind = pl.program_id(1)
    x = x_ref[...].astype(f32)
    nrow = x.shape[0]
    half = x.shape[1]
    first = (x + pos_ref[:, :half]).astype(bf16)
    second = (x + pos_ref[:, half:]).astype(bf16)
    p = jnp.dot(first, w1_ref[:half, :], preferred_element_type=f32)
    q = jnp.dot(second, w1_ref[half:, :], preferred_element_type=f32)
    hid = _gelu(p + pltpu.roll(q, nrow - 1, 0))
    out = jnp.dot(hid.astype(bf16), w2_ref[...], preferred_element_type=f32)
    out = jnp.where(kind == 0, _rms(out, gain_ref[...]), out)
    rowi = lax.broadcasted_iota(jnp.int32, out.shape, 0)
    o_ref[...] = jnp.where(rowi < nrow - 1, out, 0.0).astype(o_ref.dtype)


def _compress(xk, pos, w1, w2, gain):
    B, _, G, nrow, width = xk.shape
    hid = w1.shape[-1]
    return pl.pallas_call(
        _compress_body,
        grid=(B, 2, G),
        in_specs=[
            pl.BlockSpec((None, None, None, nrow, width), lambda b, k, g: (b, k, g, 0, 0)),
            pl.BlockSpec((None, 1, 2 * width), lambda b, k, g: (k, 0, 0)),
            pl.BlockSpec((None, 2 * width, hid), lambda b, k, g: (k, 0, 0)),
            pl.BlockSpec((None, hid, HEAD_DIM), lambda b, k, g: (k, 0, 0)),
            pl.BlockSpec((1, HEAD_DIM), lambda b, k, g: (0, 0)),
        ],
        out_specs=pl.BlockSpec((None, None, None, nrow, HEAD_DIM), lambda b, k, g: (b, k, g, 0, 0)),
        out_shape=jax.ShapeDtypeStruct((B, 2, G, nrow, HEAD_DIM), bf16),
        compiler_params=_cparams(("parallel", "parallel", "parallel")),
        name="compress",
    )(xk, pos, w1, w2, gain)


def _head_gate(gl, col):
    lane = lax.broadcasted_iota(jnp.int32, gl.shape, 1)
    return _sigmoid(jnp.sum(jnp.where(lane == col, gl, 0.0), axis=1, keepdims=True))


def _slope_table():
    sl = 2.0 ** (-8.0 * np.arange(1, N_HEADS + 1) / N_HEADS)
    tab = np.zeros((N_KV, 1, LANES), np.float32)
    tab[:, 0, :HEADS_PER_KV] = sl.reshape(N_KV, HEADS_PER_KV)
    return jnp.asarray(tab)


def _cmp_attn_body(q_ref, kc_ref, vc_ref, qg_ref, gl_ref, ov_ref, sl_ref, oc_ref, sel_ref, *, tq, n_cmp, n_sel):
    g = pl.program_id(1)
    t0 = pl.program_id(2) * tq
    q = q_ref[...].astype(f32)
    kc = kc_ref[...]
    vc = vc_ref[...]
    ncp = kc.shape[0]
    gl = gl_ref[...].astype(f32)
    t = t0 + lax.broadcasted_iota(jnp.int32, (tq, 1), 0)
    cidx = lax.broadcasted_iota(jnp.int32, (1, ncp), 1)
    cend = cidx * CMP_STRIDE + (CMP_BLOCK - 1)
    valid = (cend <= t) & (cidx < n_cmp)
    rel = (cend - t0).astype(f32)
    scale = HEAD_DIM ** -0.5
    imp = jnp.zeros((tq, LANES), f32)
    for h in range(HEADS_PER_KV):
        qh = (_rms(q[:, h * HEAD_DIM:(h + 1) * HEAD_DIM], qg_ref[...]) * scale).astype(bf16)
        s = lax.dot_general(qh, kc, (((1,), (1,)), ((), ())), preferred_element_type=f32)
        s = jnp.where(valid, s + sl_ref[:, h:h + 1] * rel, NEG)
        m = jnp.max(s, axis=1, keepdims=True)
        e = jnp.where(valid, jnp.exp(s - m), 0.0)
        den = jnp.sum(e, axis=1, keepdims=True)
        p = (e / jnp.where(den > 0.0, den, 1.0)).astype(bf16)
        o = jnp.dot(p, vc, preferred_element_type=f32)
        oc_ref[:, h * HEAD_DIM:(h + 1) * HEAD_DIM] = (o * _head_gate(gl, g * HEADS_PER_KV + h)).astype(oc_ref.dtype)
        imp = imp + jnp.dot(p, ov_ref[...], preferred_element_type=f32)

    j = lax.broadcasted_iota(jnp.int32, (1, LANES), 1)
    cur = t // SEL_BLOCK
    forced = (j == 0) | (j == cur) | (j == cur - 1)
    future = j * SEL_BLOCK > t
    imp = jnp.where(forced, FORCE, jnp.where(future, NEG, imp))
    imp = jnp.where(j < n_sel, imp, 2.0 * NEG)
    imp_t = imp.T[0:n_sel]
    row8 = lax.broadcasted_iota(jnp.int32, (SUBLANES, 1), 0)
    groups = [imp_t[r * SUBLANES:(r + 1) * SUBLANES] for r in range(n_sel // SUBLANES)]
    rows = []
    for jj in range(n_sel):
        v = imp_t[jj:jj + 1, :]
        part = jnp.zeros((SUBLANES, tq), f32)
        for r, blk in enumerate(groups):
            lo = r * SUBLANES
            if lo + SUBLANES <= jj:
                ahead = jnp.where(blk >= v, 1.0, 0.0)
            elif lo > jj:
                ahead = jnp.where(blk > v, 1.0, 0.0)
            else:
                ahead = jnp.where(row8 < jj - lo, jnp.where(blk >= v, 1.0, 0.0), jnp.where(blk > v, 1.0, 0.0))
            part = part + ahead
        rows.append(jnp.sum(part, axis=0, keepdims=True))
    rank = jnp.concatenate(rows, axis=0)
    neg_t = jnp.where(rank < float(min(SEL_TOPK, n_sel)), 0.0, NEG)
    pieces = [jnp.zeros((SEL_LANE0, tq), f32), neg_t]
    if SEL_LANE0 + n_sel < LANES:
        pieces.append(jnp.zeros((LANES - SEL_LANE0 - n_sel, tq), f32))
    sel_ref[...] = jnp.concatenate(pieces, axis=0).T.astype(sel_ref.dtype)


def _cmp_attn(proj, gl, kvc, q_gain, overlap, B, S, q_col0, *, tq):
    G = N_KV
    ncp = kvc.shape[3]
    n_cmp = (S - CMP_BLOCK) // CMP_STRIDE + 1
    n_sel = S // SEL_BLOCK
    nq = S // tq
    gw = HEADS_PER_KV * HEAD_DIM
    qb0 = q_col0 // gw
    assert SEL_LANE0 + n_sel <= LANES
    oc, sel = pl.pallas_call(
        functools.partial(_cmp_attn_body, tq=tq, n_cmp=n_cmp, n_sel=n_sel),
        grid=(B, G, nq),
        in_specs=[
            pl.BlockSpec((tq, gw), lambda b, g, i: (b * nq + i, qb0 + g)),
            pl.BlockSpec((None, None, None, ncp, HEAD_DIM), lambda b, g, i: (b, 0, g, 0, 0)),
            pl.BlockSpec((None, None, None, ncp, HEAD_DIM), lambda b, g, i: (b, 1, g, 0, 0)),
            pl.BlockSpec((1, HEAD_DIM), lambda b, g, i: (0, 0)),
            pl.BlockSpec((tq, LANES), lambda b, g, i: (b * nq + i, 0)),
            pl.BlockSpec((ncp, LANES), lambda b, g, i: (0, 0)),
            pl.BlockSpec((None, 1, LANES), lambda b, g, i: (g, 0, 0)),
        ],
        out_specs=[
            pl.BlockSpec((tq, gw), lambda b, g, i: (b * nq + i, g)),
            pl.BlockSpec((None, None, tq, LANES), lambda b, g, i: (b, g, i, 0)),
        ],
        out_shape=[jax.ShapeDtypeStruct((B * S, N_HEADS * HEAD_DIM), bf16),
                   jax.ShapeDtypeStruct((B, G, S, LANES), bf16)],
        compiler_params=_cparams(("parallel", "parallel", "parallel")),
        name="cmp_attn",
    )(proj, kvc, kvc, q_gain, gl, overlap, _slope_table())
    return oc, sel


def _sel_win_body(q_ref, ks_ref, vs_ref, kw_ref, vw_ref, qg_ref, kg_ref, gl_ref, sel_ref, oc_ref, sl_ref, o_ref,
                  ksa_ref, kwa_ref, s_ref, mx_ref, l_ref, acc_ref, *, tq):
    g = pl.program_id(1)
    qi = pl.program_id(2)
    hk = HEADS_PER_KV
    nw = WINDOW // tq
    nl = tq // LANES
    nt = (((1,), (1,)), ((), ()))

    @pl.when(qi == 0)
    def _():
        n_keys = ks_ref.shape[0]
        kpos = lax.broadcasted_iota(jnp.int32, (n_keys, LANES), 0)
        lane = lax.broadcasted_iota(jnp.int32, (n_keys, LANES), 1)
        blk = kpos // SEL_BLOCK
        pos_cols = jnp.where(lane < N_SLOPE_PARTS, blk.astype(f32),
                             jnp.where(lane < 2 * N_SLOPE_PARTS, (kpos % SEL_BLOCK).astype(f32), 0.0))
        onehot = jnp.where(lane - SEL_LANE0 == blk, 1.0, 0.0)
        ksa_ref[:, :HEAD_DIM] = _rms(ks_ref[...].astype(f32), kg_ref[0:1, :]).astype(bf16)
        ksa_ref[:, HEAD_DIM:] = (pos_cols + onehot).astype(bf16)
        kwa_ref[:, :HEAD_DIM] = _rms(kw_ref[...].astype(f32), kg_ref[1:2, :]).astype(bf16)
        kwa_ref[:, HEAD_DIM:] = pos_cols.astype(bf16)

    q = q_ref[...].astype(f32)
    scale = HEAD_DIM ** -0.5
    qx = sel_ref[...].astype(f32)
    qa = jnp.concatenate(
        [jnp.concatenate([(_rms(q[:, h * HEAD_DIM:(h + 1) * HEAD_DIM], qg_ref[...]) * scale).astype(bf16),
                          (qx + sl_ref[h:h + 1, :]).astype(bf16)], axis=1) for h in range(hk)],
        axis=0)
    r = lax.broadcasted_iota(jnp.int32, (tq, tq), 0)
    c = lax.broadcasted_iota(jnp.int32, (tq, tq), 1)
    mask_diag = jnp.concatenate([jnp.where(c <= r, 0.0, NEG)] * hk, axis=0)
    mask_edge = jnp.concatenate([jnp.where(c > r, 0.0, NEG)] * hk, axis=0)

    def reset():
        mx_ref[...] = jnp.full_like(mx_ref, NEG)
        l_ref[...] = jnp.zeros_like(l_ref)
        acc_ref[...] = jnp.zeros_like(acc_ref)

    def score_tile(ka_ref, kt, slot, mask):
        k0 = pl.multiple_of(kt * tq, tq)
        s = lax.dot_general(qa, ka_ref[pl.ds(k0, tq), :], nt, preferred_element_type=f32)
        if mask is not None:
            s = s + mask
        s_ref[slot] = s
        part = s[:, 0:LANES]
        for i in range(1, nl):
            part = jnp.maximum(part, s[:, i * LANES:(i + 1) * LANES])
        mx_ref[...] = jnp.maximum(mx_ref[...], part)

    def value_tile(v_ref, kt, slot, m):
        k0 = pl.multiple_of(kt * tq, tq)
        p = jnp.exp(s_ref[slot] - m)
        part = p[:, 0:LANES]
        for i in range(1, nl):
            part = part + p[:, i * LANES:(i + 1) * LANES]
        l_ref[...] += part
        acc_ref[...] += jnp.dot(p.astype(bf16), v_ref[pl.ds(k0, tq), :], preferred_element_type=f32)

    def finish():
        return acc_ref[...] / jnp.sum(l_ref[...], axis=1, keepdims=True)

    reset()

    def sel_scores(kt, carry):
        score_tile(ksa_ref, kt, kt, None)
        return carry

    lax.fori_loop(0, qi, sel_scores, 0)
    score_tile(ksa_ref, qi, qi, mask_diag)
    m_sel = jnp.max(mx_ref[...], axis=1, keepdims=True)

    def sel_values(kt, carry):
        value_tile(vs_ref, kt, kt, m_sel)
        return carry

    lax.fori_loop(0, qi + 1, sel_values, 0)
    o_sel = finish()

    reset()
    for d in range(nw, 0, -1):
        mask = mask_edge if d == nw else None
        pl.when(qi >= d)(functools.partial(score_tile, kwa_ref, qi - d, nw - d, mask))
    score_tile(kwa_ref, qi, nw, mask_diag)
    m_win = jnp.max(mx_ref[...], axis=1, keepdims=True)
    for d in range(nw, 0, -1):
        pl.when(qi >= d)(functools.partial(value_tile, vw_ref, qi - d, nw - d, m_win))
    value_tile(vw_ref, qi, nw, m_win)
    o_win = finish()

    gl = gl_ref[...].astype(f32)
    for h in range(hk):
        col = g * hk + h
        rows = slice(h * tq, (h + 1) * tq)
        o = (oc_ref[:, h * HEAD_DIM:(h + 1) * HEAD_DIM].astype(f32)
             + _head_gate(gl, N_HEADS + col) * o_sel[rows] + _head_gate(gl, 2 * N_HEADS + col) * o_win[rows])
        o_ref[:, h * HEAD_DIM:(h + 1) * HEAD_DIM] = o.astype(o_ref.dtype)


def _alibi_parts_table():
    sl = (2.0 ** (-8.0 * np.arange(1, N_HEADS + 1) / N_HEADS)).astype(np.float32)
    tab = np.zeros((N_HEADS, LANES), np.float32)
    rest = sl
    for i in range(N_SLOPE_PARTS):
        piece = rest.astype(bf16).astype(np.float32)
        rest = rest - piece
        tab[:, i] = piece * SEL_BLOCK
        tab[:, N_SLOPE_PARTS + i] = piece
    return jnp.asarray(tab.reshape(N_KV, HEADS_PER_KV, LANES))


def _sel_win_attn(proj, gl, sel, oc, q_gain, k_gains2, B, S, q_col0, kv_col0, *, tq):
    G = N_KV
    nq = S // tq
    gw = HEADS_PER_KV * HEAD_DIM
    qb0 = q_col0 // gw
    kb0 = kv_col0 // HEAD_DIM
    assert WINDOW % tq == 0 and tq % LANES == 0 and 2 * N_SLOPE_PARTS <= SEL_LANE0
    assert S // SEL_BLOCK <= 256
    n_slots = max(nq, WINDOW // tq + 1)

    def kv_spec(kind):
        return pl.BlockSpec((S, HEAD_DIM), lambda b, g, i, kind=kind: (b, kb0 + kind * G + g))

    return pl.pallas_call(
        functools.partial(_sel_win_body, tq=tq),
        grid=(B, G, nq),
        in_specs=[
            pl.BlockSpec((tq, gw), lambda b, g, i: (b * nq + i, qb0 + g)),
            kv_spec(2), kv_spec(3), kv_spec(4), kv_spec(5),
            pl.BlockSpec((1, HEAD_DIM), lambda b, g, i: (0, 0)),
            pl.BlockSpec((2, HEAD_DIM), lambda b, g, i: (0, 0)),
            pl.BlockSpec((tq, LANES), lambda b, g, i: (b * nq + i, 0)),
            pl.BlockSpec((None, None, tq, LANES), lambda b, g, i: (b, g, i, 0)),
            pl.BlockSpec((tq, gw), lambda b, g, i: (b * nq + i, g)),
            pl.BlockSpec((None, HEADS_PER_KV, LANES), lambda b, g, i: (g, 0, 0)),
        ],
        out_specs=pl.BlockSpec((tq, gw), lambda b, g, i: (b * nq + i, g)),
        out_shape=jax.ShapeDtypeStruct((B * S, N_HEADS * HEAD_DIM), bf16),
        scratch_shapes=[pltpu.VMEM((S, 2 * HEAD_DIM), bf16), pltpu.VMEM((S, 2 * HEAD_DIM), bf16),
                        pltpu.VMEM((n_slots, HEADS_PER_KV * tq, tq), f32),
                        pltpu.VMEM((HEADS_PER_KV * tq, LANES), f32), pltpu.VMEM((HEADS_PER_KV * tq, LANES), f32),
                        pltpu.VMEM((HEADS_PER_KV * tq, HEAD_DIM), f32)],
        compiler_params=_cparams(("parallel", "parallel", "arbitrary")),
        name="sel_win_attn",
    )(proj, proj, proj, proj, proj, q_gain, k_gains2, gl, sel, oc, _alibi_parts_table())


def _router_body(x_ref, gain_ref, wr_ref, br_ref, xn_ref, info_ref, cnt_ref, carry_ref, *, tm, n_exp):
    @pl.when(pl.program_id(0) == 0)
    def _():
        carry_ref[...] = jnp.zeros_like(carry_ref)

    xn = _rms(x_ref[...], gain_ref[...])
    xn_ref[...] = xn
    lane = lax.broadcasted_iota(jnp.int32, (tm, LANES), 1)
    x_hi = xn.astype(bf16)
    x_lo = (xn - x_hi.astype(f32)).astype(bf16)
    logits = (jnp.dot(x_hi, wr_ref[0], preferred_element_type=f32)
              + jnp.dot(x_lo, wr_ref[0], preferred_element_type=f32)
              + jnp.dot(x_hi, wr_ref[1], preferred_element_type=f32)) + br_ref[...]
    logits = jnp.where(lane < n_exp, logits, 2.0 * NEG)
    hots, vals = [], []
    for _ in range(TOP_K):
        mx = jnp.max(logits, axis=1, keepdims=True)
        idx = jnp.min(jnp.where(logits == mx, lane, LANES), axis=1, keepdims=True)
        hot = lane == idx
        hots.append(hot)
        vals.append(mx)
        logits = jnp.where(hot, 2.0 * NEG, logits)
    es = [jnp.exp(v - vals[0]) for v in vals]
    den = es[0] + es[1] + es[2] + es[3]
    onehot = jnp.zeros((tm, LANES), f32)
    for hot in hots:
        onehot = onehot + hot.astype(f32)
    r = lax.broadcasted_iota(jnp.int32, (tm, tm), 0)
    c = lax.broadcasted_iota(jnp.int32, (tm, tm), 1)
    before = jnp.dot((c < r).astype(bf16), onehot.astype(bf16), preferred_element_type=f32)
    rank = carry_ref[...] + before
    carry_ref[...] = carry_ref[...] + jnp.sum(onehot, axis=0, keepdims=True)
    cnt_ref[...] = carry_ref[...]
    info = jnp.zeros((tm, LANES), f32)
    for k in range(TOP_K):
        e_k = jnp.sum(jnp.where(hots[k], lane, 0), axis=1, keepdims=True).astype(f32)
        r_k = jnp.sum(jnp.where(hots[k], rank, 0.0), axis=1, keepdims=True)
        info = jnp.where(lane == k, e_k, info)
        info = jnp.where(lane == TOP_K + k, es[k] / den, info)
        info = jnp.where(lane == 2 * TOP_K + k, r_k, info)
    info_ref[...] = info


def _router(x, gain, wr, br, *, tm):
    T, D = x.shape
    return pl.pallas_call(
        functools.partial(_router_body, tm=tm, n_exp=N_EXPERTS),
        grid=(T // tm,),
        in_specs=[
            pl.BlockSpec((tm, D), lambda i: (i, 0)),
            pl.BlockSpec((1, D), lambda i: (0, 0)),
            pl.BlockSpec((2, D, LANES), lambda i: (0, 0, 0)),
            pl.BlockSpec((1, LANES), lambda i: (0, 0)),
        ],
        out_specs=[
            pl.BlockSpec((tm, D), lambda i: (i, 0)),
            pl.BlockSpec((tm, LANES), lambda i: (i, 0)),
            pl.BlockSpec((1, LANES), lambda i: (0, 0)),
        ],
        out_shape=[jax.ShapeDtypeStruct((T, D), f32), jax.ShapeDtypeStruct((T, LANES), f32),
                   jax.ShapeDtypeStruct((1, LANES), f32)],
        scratch_shapes=[pltpu.VMEM((1, LANES), f32)],
        compiler_params=_cparams(("arbitrary",)),
        name="router",
    )(x, gain, wr, br)


def _dispatch_body(pos_ref, xn_ref, xs_in_ref, xs_ref, sem, *, tm):
    del xs_in_ref

    def row_copy(t, k):
        return pltpu.make_async_copy(xn_ref.at[pl.ds(t, 1)], xs_ref.at[pl.ds(pos_ref[t * TOP_K + k], 1)], sem)

    def issue(t, c):
        for k in range(TOP_K):
            row_copy(t, k).start()
        return c

    def drain(t, c):
        for k in range(TOP_K):
            row_copy(t, k).wait()
        return c

    lax.fori_loop(0, tm, issue, 0)
    lax.fori_loop(0, tm, drain, 0)


def _dispatch(pos_flat, xn, n_rows, *, tm):
    T, D = xn.shape
    xs0 = jnp.zeros((n_rows, D), xn.dtype)
    return pl.pallas_call(
        functools.partial(_dispatch_body, tm=tm),
        grid=(T // tm,),
        in_specs=[
            pl.BlockSpec((tm * TOP_K,), lambda i: (i,), memory_space=pltpu.SMEM),
            pl.BlockSpec((tm, D), lambda i: (i, 0)),
            pl.BlockSpec(memory_space=pl.ANY),
        ],
        out_specs=pl.BlockSpec(memory_space=pl.ANY),
        out_shape=jax.ShapeDtypeStruct((n_rows, D), xn.dtype),
        scratch_shapes=[pltpu.SemaphoreType.DMA],
        input_output_aliases={2: 0},
        compiler_params=_cparams(("arbitrary",), disable_bounds_checks=True, has_side_effects=True),
        name="moe_dispatch",
    )(pos_flat, xn, xs0)


def _new_expert(be_ref, i):
    return (i == 0) | (be_ref[i] != be_ref[jnp.maximum(i - 1, 0)])


def _up_body(be_ref, nu_ref, x_ref, wg_ref, wl_ref, bg_ref, bl_ref, h_ref, wgs_ref, wls_ref):
    i = pl.program_id(1)

    @pl.when(_new_expert(be_ref, i) & (i < nu_ref[0]))
    def _():
        wgs_ref[...] = wg_ref[...].astype(bf16)
        wls_ref[...] = wl_ref[...].astype(bf16)

    @pl.when(i < nu_ref[0])
    def _():
        x = x_ref[...].astype(bf16)
        glu = jnp.minimum(jnp.dot(x, wgs_ref[...], preferred_element_type=f32) + bg_ref[...], SWIGLU_LIMIT)
        lin = jnp.clip(jnp.dot(x, wls_ref[...], preferred_element_type=f32) + bl_ref[...], -SWIGLU_LIMIT,
                       SWIGLU_LIMIT)
        h_ref[...] = (glu * _sigmoid(SWIGLU_ALPHA * glu) * (lin + 1.0)).astype(h_ref.dtype)

    @pl.when(i >= nu_ref[0])
    def _():
        h_ref[...] = jnp.zeros_like(h_ref)


def _moe_up(block_e, n_used, xs, w_up, b_up, *, rb, tf):
    R, D = xs.shape
    d_ff = w_up.shape[2] // 2
    nj = d_ff // tf
    nb = R // rb
    rowmap = lambda j, i, be, nu: (jnp.minimum(i, nu[0] - 1), 0)
    return pl.pallas_call(
        _up_body,
        grid_spec=pltpu.PrefetchScalarGridSpec(
            num_scalar_prefetch=2,
            grid=(nj, nb),
            in_specs=[
                pl.BlockSpec((rb, D), rowmap),
                pl.BlockSpec((None, D, tf), lambda j, i, be, nu: (be[i], 0, j)),
                pl.BlockSpec((None, D, tf), lambda j, i, be, nu: (be[i], 0, nj + j)),
                pl.BlockSpec((None, 1, tf), lambda j, i, be, nu: (be[i], 0, j)),
                pl.BlockSpec((None, 1, tf), lambda j, i, be, nu: (be[i], 0, nj + j)),
            ],
            out_specs=pl.BlockSpec((rb, tf), lambda j, i, be, nu: (i, j)),
            scratch_shapes=[pltpu.VMEM((D, tf), bf16), pltpu.VMEM((D, tf), bf16)],
        ),
        out_shape=jax.ShapeDtypeStruct((R, d_ff), bf16),
        compiler_params=pltpu.CompilerParams(dimension_semantics=("arbitrary", "arbitrary"),
                                             vmem_limit_bytes=MOE_VMEM_LIMIT),
        name="moe_up",
    )(block_e, n_used, xs, w_up, w_up, b_up, b_up)


def _down_body(be_ref, nu_ref, h_ref, w_ref, b_ref, y_ref, ws_ref):
    i = pl.program_id(1)

    @pl.when(_new_expert(be_ref, i) & (i < nu_ref[0]))
    def _():
        ws_ref[...] = w_ref[...].astype(bf16)

    @pl.when(i < nu_ref[0])
    def _():
        y_ref[...] = jnp.dot(h_ref[...], ws_ref[...], preferred_element_type=f32) + b_ref[...]

    @pl.when(i >= nu_ref[0])
    def _():
        y_ref[...] = jnp.zeros_like(y_ref)


def _moe_down(block_e, n_used, h, w_down, b_down, *, rb, tn):
    R, F = h.shape
    D = w_down.shape[2]
    return pl.pallas_call(
        _down_body,
        grid_spec=pltpu.PrefetchScalarGridSpec(
            num_scalar_prefetch=2,
            grid=(D // tn, R // rb),
            in_specs=[
                pl.BlockSpec((rb, F), lambda j, i, be, nu: (jnp.minimum(i, nu[0] - 1), 0)),
                pl.BlockSpec((None, F, tn), lambda j, i, be, nu: (be[i], 0, j)),
                pl.BlockSpec((None, 1, tn), lambda j, i, be, nu: (be[i], 0, j)),
            ],
            out_specs=pl.BlockSpec((rb, tn), lambda j, i, be, nu: (i, j)),
            scratch_shapes=[pltpu.VMEM((F, tn), bf16)],
        ),
        out_shape=jax.ShapeDtypeStruct((R, D), f32),
        compiler_params=pltpu.CompilerParams(dimension_semantics=("arbitrary", "arbitrary"),
                                             vmem_limit_bytes=MOE_VMEM_LIMIT),
        name="moe_down",
    )(block_e, n_used, h, w_down, b_down)


def _combine_body(pos_ref, info_ref, x_ref, ys_ref, o_ref, buf_ref, sem, *, tm):
    def row_copy(t, k):
        return pltpu.make_async_copy(ys_ref.at[pl.ds(pos_ref[t * TOP_K + k], 1)], buf_ref.at[k, pl.ds(t, 1)], sem)

    def issue(t, c):
        for k in range(TOP_K):
            row_copy(t, k).start()
        return c

    def drain(t, c):
        for k in range(TOP_K):
            row_copy(t, k).wait()
        return c

    lax.fori_loop(0, tm, issue, 0)
    lax.fori_loop(0, tm, drain, 0)
    info = info_ref[...]
    lane = lax.broadcasted_iota(jnp.int32, info.shape, 1)
    y = x_ref[...]
    for k in range(TOP_K):
        gate = jnp.sum(jnp.where(lane == TOP_K + k, info, 0.0), axis=1, keepdims=True)
        y = y + gate * buf_ref[k]
    o_ref[...] = y


def _combine(pos_flat, info, x, ys, *, tm):
    T, D = x.shape
    return pl.pallas_call(
        functools.partial(_combine_body, tm=tm),
        grid=(T // tm,),
        in_specs=[
            pl.BlockSpec((tm * TOP_K,), lambda i: (i,), memory_space=pltpu.SMEM),
            pl.BlockSpec((tm, LANES), lambda i: (i, 0)),
            pl.BlockSpec((tm, D), lambda i: (i, 0)),
            pl.BlockSpec(memory_space=pl.ANY),
        ],
        out_specs=pl.BlockSpec((tm, D), lambda i: (i, 0)),
        out_shape=jax.ShapeDtypeStruct((T, D), f32),
        scratch_shapes=[pltpu.VMEM((TOP_K, tm, D), f32), pltpu.SemaphoreType.DMA],
        compiler_params=_cparams(("arbitrary",), disable_bounds_checks=True),
        name="moe_combine",
    )(pos_flat, info, x, ys)


def _pick(n, *cands):
    for c in cands:
        if n % c == 0:
            return c
    return n


def _layer(x2d, p2d, B, S, norm_mix, w_in, conv_w, conv_b, w_rg_a, b_rg_a, w_rg_i, b_rg_i, lru_lambda, w_rnn_out,
           q_gain, k_gains, cmp_pos_k, cmp_w1_k, cmp_w2_k, cmp_pos_v, cmp_w1_v, cmp_w2_v, w_nsa_out, w_out,
           norm_moe, w_router, b_router, w_up, b_up, w_down, b_down, w_ple_proj, ple_norm, w_ple_gate):
    T, D = x2d.shape
    d_rnn = conv_w.shape[1]
    q_w = N_HEADS * HEAD_DIM
    kv_w = 6 * N_KV * HEAD_DIM
    n_gate = 3 * N_HEADS
    q0 = 2 * d_rnn
    kv0 = q0 + q_w
    gn0 = kv0 + kv_w
    gm0 = gn0 + n_gate
    tm = _pick(T, 1024, 512, 256)

    w_main = jnp.concatenate([w_in[:, :gn0], w_in[:, gm0:]], axis=1).astype(bf16)
    w_gate = jnp.pad(w_in[:, gn0:gm0], ((0, 0), (0, LANES - n_gate))).astype(bf16)
    proj = _matmul(x2d, w_main, tm=tm, tn=1024, out_dtype=bf16, gain=norm_mix, name="in_proj")
    gl = _matmul(x2d, w_gate, tm=tm, tn=LANES, out_dtype=f32, gain=norm_mix, name="in_proj_gates")
    merge0 = gn0

    sp = jax.nn.softplus(-lru_lambda.astype(f32)).reshape(1, d_rnn)
    hg = _rglru(proj, B, S, d_rnn, conv_w, conv_b.reshape(1, d_rnn), _super_blocks(w_rg_a),
                b_rg_a.reshape(1, d_rnn), _super_blocks(w_rg_i), b_rg_i.reshape(1, d_rnn), sp,
                tc=_pick(S, 256, 128))

    nrow = S // CMP_STRIDE
    kv_cmp = proj[:, kv0:kv0 + 2 * N_KV * HEAD_DIM].reshape(B, nrow, CMP_STRIDE, 2, N_KV, HEAD_DIM)
    kv_cmp = kv_cmp.transpose(0, 3, 4, 1, 2, 5).reshape(B, 2, N_KV, nrow, CMP_STRIDE * HEAD_DIM)
    pos = jnp.stack([cmp_pos_k, cmp_pos_v]).reshape(2, 1, CMP_BLOCK * HEAD_DIM).astype(f32)
    w1 = jnp.stack([cmp_w1_k, cmp_w1_v]).astype(bf16)
    w2 = jnp.stack([cmp_w2_k, cmp_w2_v]).astype(bf16)
    kvc = _compress(kv_cmp, pos, w1, w2, k_gains[0:1].astype(f32))

    n_cmp = (S - CMP_BLOCK) // CMP_STRIDE + 1
    n_sel = S // SEL_BLOCK
    c0 = np.arange(nrow)[:, None] * CMP_STRIDE
    s0 = np.arange(LANES)[None, :] * SEL_BLOCK
    overlap = np.clip(np.minimum(c0 + CMP_BLOCK, s0 + SEL_BLOCK) - np.maximum(c0, s0), 0, None) / CMP_STRIDE
    overlap = overlap * (np.arange(nrow)[:, None] < n_cmp) * (np.arange(LANES)[None, :] < n_sel)
    qg = q_gain.reshape(1, HEAD_DIM).astype(f32)
    oc, sel = _cmp_attn(proj, gl, kvc, qg, jnp.asarray(overlap, bf16), B, S, q0, tq=_pick(S, 256, 128))
    o_nsa = _sel_win_attn(proj, gl, sel, oc, qg, k_gains[1:3].astype(f32), B, S, q0, kv0,
                          tq=_pick(S, 256, 128))

    y_nsa = _matmul(o_nsa, w_nsa_out.astype(bf16), tm=tm, tn=1024, out_dtype=bf16,
                    extras=[(proj, merge0 + D)],
                    epilogue=lambda acc, m: _sigmoid(m.astype(f32)) * acc, name="nsa_out")
    merged = _matmul(hg, w_rnn_out.astype(bf16), tm=tm, tn=1024, out_dtype=bf16,
                     extras=[(proj, merge0), (y_nsa, 0)],
                     epilogue=lambda acc, m, y: _sigmoid(m.astype(f32)) * acc + y.astype(f32), name="rnn_out")
    x1 = _matmul(merged, w_out.astype(bf16), tm=tm, tn=1024, out_dtype=f32, extras=[(x2d, 0)],
                 epilogue=lambda acc, xr: xr + acc, name="mix_out")

    x2 = _moe(x1, norm_moe, w_router, b_router, w_up, b_up, w_down, b_down)

    ple_gain = ple_norm.reshape(1, D).astype(f32)
    ple = _ple_proj(p2d, w_ple_proj.astype(bf16), ple_gain, tm=_pick(T, 512, 256))
    return _matmul(x2, w_ple_gate.astype(bf16), tm=tm, tn=1024, out_dtype=f32, extras=[(x2, 0), (ple, 0)],
                   epilogue=lambda acc, xr, pe: xr + _sigmoid(acc) * pe.astype(f32), name="ple_gate")


def _moe(x1, norm_moe, w_router, b_router, w_up, b_up, w_down, b_down):
    T, D = x1.shape
    wr_f = jnp.pad(w_router.astype(f32), ((0, 0), (0, LANES - N_EXPERTS)))
    wr_hi = wr_f.astype(bf16)
    wr = jnp.stack([wr_hi, (wr_f - wr_hi.astype(f32)).astype(bf16)])
    br = jnp.pad(b_router, (0, LANES - N_EXPERTS)).reshape(1, LANES).astype(f32)
    xn, info, cnt = _router(x1, norm_moe.reshape(1, D).astype(f32), wr, br, tm=_pick(T, 512, 256))
    rb = MOE_ROWS
    eid = info[:, 0:TOP_K].astype(jnp.int32)
    rank = info[:, 2 * TOP_K:3 * TOP_K].astype(jnp.int32)
    counts = cnt[0, :N_EXPERTS].astype(jnp.int32)
    padded = (counts + rb - 1) // rb * rb
    pend = jnp.cumsum(padded)
    pstart = pend - padded
    pos_flat = (pstart[eid] + rank).reshape(T * TOP_K)
    n_blocks = (T * TOP_K) // rb + N_EXPERTS
    block_e = jnp.minimum(jnp.sum(jnp.arange(n_blocks)[:, None] * rb >= pend[None, :], axis=1),
                          N_EXPERTS - 1).astype(jnp.int32)
    n_used = (pend[-1:] // rb).astype(jnp.int32)
    xs = _dispatch(pos_flat, xn, n_blocks * rb, tm=_pick(T, 256))
    hmid = _moe_up(block_e, n_used, xs, w_up, b_up.reshape(N_EXPERTS, 1, -1).astype(f32), rb=rb, tf=1024)
    ys = _moe_down(block_e, n_used, hmid, w_down, b_down.reshape(N_EXPERTS, 1, -1).astype(f32), rb=rb,
                   tn=w_down.shape[2])
    return _combine(pos_flat, info, x1, ys, tm=_pick(T, 256))


def _ple_body(p_ref, w_ref, g_ref, o_ref):
    acc = jnp.dot(p_ref[...].astype(bf16), w_ref[...], preferred_element_type=f32)
    o_ref[...] = _rms(acc, g_ref[...]).astype(o_ref.dtype)


def _ple_proj(p2d, w, gain, *, tm):
    T, K = p2d.shape
    D = w.shape[1]
    return pl.pallas_call(
        _ple_body,
        grid=(T // tm,),
        in_specs=[pl.BlockSpec((tm, K), lambda i: (i, 0)), pl.BlockSpec((K, D), lambda i: (0, 0)),
                  pl.BlockSpec((1, D), lambda i: (0, 0))],
        out_specs=pl.BlockSpec((tm, D), lambda i: (i, 0)),
        out_shape=jax.ShapeDtypeStruct((T, D), bf16),
        compiler_params=_cparams(("parallel",)),
        name="ple_proj",
    )(p2d, w, gain)


def kernel(x, p, norm_mix, w_in, conv_w, conv_b, w_rg_a, b_rg_a, w_rg_i, b_rg_i, lru_lambda, w_rnn_out, q_gain, k_gains, cmp_pos_k, cmp_w1_k, cmp_w2_k, cmp_pos_v, cmp_w1_v, cmp_w2_v, w_nsa_out, w_out, norm_moe, w_router, b_router, w_up, b_up, w_down, b_down, w_ple_proj, ple_norm, w_ple_gate):
    B, S, D = x.shape
    depth = p.shape[0]
    x2d = x.reshape(B * S, D)
    for i in range(depth):
        x2d = _layer(x2d, p[i].reshape(B * S, -1), B, S, norm_mix[i], w_in[i], conv_w[i], conv_b[i], w_rg_a[i],
                     b_rg_a[i], w_rg_i[i], b_rg_i[i], lru_lambda[i], w_rnn_out[i], q_gain[i], k_gains[i],
                     cmp_pos_k[i], cmp_w1_k[i], cmp_w2_k[i], cmp_pos_v[i], cmp_w1_v[i], cmp_w2_v[i], w_nsa_out[i],
                     w_out[i], norm_moe[i], w_router[i], b_router[i], w_up[i], b_up[i], w_down[i], b_down[i],
                     w_ple_proj[i], ple_norm[i], w_ple_gate[i])
    return x2d.reshape(B, S, D)
```

```python
import functools
import math

import numpy as np
import jax
import jax.numpy as jnp
from jax import lax
from jax.experimental import pallas as pl
from jax.experimental.pallas import tpu as pltpu

RNN_BLOCKS = 16
CONV_WIDTH = 4
LRU_C = 8.0
N_HEADS = 16
HEAD_DIM = 128
N_KV = 4
HEADS_PER_KV = N_HEADS // N_KV
CMP_BLOCK = 32
CMP_STRIDE = 16
SEL_BLOCK = 64
SEL_TOPK = 16
WINDOW = 512
N_EXPERTS = 32
TOP_K = 4
SWIGLU_LIMIT = 7.0
SWIGLU_ALPHA = 1.702
EPS = 1e-6
NEG = -1e30
FORCE = 1e9

LANES = 128
SUBLANES = 8
RNN_SUPER = 640
VMEM_LIMIT = 56 * 1024 * 1024
MOE_VMEM_LIMIT = 60 * 1024 * 1024
MOE_ROWS = 512
SEL_LANE0 = 64
N_SLOPE_PARTS = 3

bf16 = jnp.bfloat16
f32 = jnp.float32


def _cparams(sem, **kw):
    return pltpu.CompilerParams(dimension_semantics=sem, vmem_limit_bytes=VMEM_LIMIT, **kw)


def _gelu(x):
    return 0.5 * x * (1.0 + jnp.tanh(math.sqrt(2.0 / math.pi) * (x + 0.044715 * (x * x * x))))


def _sigmoid(x):
    return 0.5 * jnp.tanh(0.5 * x) + 0.5


def _rms(x, gain):
    return x * lax.rsqrt(jnp.mean(x * x, axis=-1, keepdims=True) + EPS) * gain


def _mm_body(*refs, has_gain, n_extra, epilogue, stage):
    i = 0
    a_ref = refs[i]; i += 1
    gain_ref = None
    if has_gain:
        gain_ref = refs[i]; i += 1
    w_ref = refs[i]; i += 1
    extra = refs[i:i + n_extra]; i += n_extra
    o_ref = refs[i]; i += 1
    if stage:
        h_ref = refs[i]

        @pl.when(pl.program_id(1) == 0)
        def _():
            a = a_ref[...].astype(f32)
            if has_gain:
                a = _rms(a, gain_ref[...])
            h_ref[...] = a.astype(bf16)

        lhs = h_ref[...]
    else:
        lhs = a_ref[...]
    acc = jnp.dot(lhs, w_ref[...], preferred_element_type=f32)
    o_ref[...] = epilogue(acc, *[e[...] for e in extra]).astype(o_ref.dtype)


def _matmul(a, w, *, tm, tn, out_dtype, gain=None, extras=(), epilogue=None, name="mm"):
    M, K = a.shape
    N = w.shape[1]
    assert M % tm == 0 and N % tn == 0
    stage = gain is not None or a.dtype != bf16
    if epilogue is None:
        epilogue = lambda acc: acc
    in_specs = [pl.BlockSpec((tm, K), lambda i, j: (i, 0))]
    args = [a]
    if gain is not None:
        in_specs.append(pl.BlockSpec((1, K), lambda i, j: (0, 0)))
        args.append(gain.reshape(1, K).astype(f32))
    in_specs.append(pl.BlockSpec((K, tn), lambda i, j: (0, j)))
    args.append(w)
    for arr, col0 in extras:
        assert col0 % tn == 0
        off = col0 // tn
        in_specs.append(pl.BlockSpec((tm, tn), lambda i, j, off=off: (i, j + off)))
        args.append(arr)
    body = functools.partial(_mm_body, has_gain=gain is not None, n_extra=len(extras), epilogue=epilogue,
                             stage=stage)
    return pl.pallas_call(
        body,
        grid=(M // tm, N // tn),
        in_specs=in_specs,
        out_specs=pl.BlockSpec((tm, tn), lambda i, j: (i, j)),
        out_shape=jax.ShapeDtypeStruct((M, N), out_dtype),
        scratch_shapes=[pltpu.VMEM((tm, K), bf16)] if stage else [],
        compiler_params=_cparams(("parallel", "arbitrary")),
        name=name,
    )(*args)


def _branch_merge_body(hg_ref, wr_ref, o_ref, wn_ref, m0_ref, m1_ref, out_ref):
    y_rnn = jnp.dot(hg_ref[...], wr_ref[...], preferred_element_type=f32)
    y_nsa = jnp.dot(o_ref[...], wn_ref[...], preferred_element_type=f32)
    out_ref[...] = (_sigmoid(m0_ref[...].astype(f32)) * y_rnn
                    + _sigmoid(m1_ref[...].astype(f32)) * y_nsa).astype(out_ref.dtype)


def _branch_merge(hg, w_rnn, o_nsa, w_nsa, proj, merge0, *, tm, tn):
    M, kr = hg.shape
    kn = o_nsa.shape[1]
    N = w_rnn.shape[1]
    assert M % tm == 0 and N % tn == 0 and merge0 % tn == 0
    off = merge0 // tn
    return pl.pallas_call(
        _branch_merge_body,
        grid=(M // tm, N // tn),
        in_specs=[
            pl.BlockSpec((tm, kr), lambda i, j: (i, 0)),
            pl.BlockSpec((kr, tn), lambda i, j: (0, j)),
            pl.BlockSpec((tm, kn), lambda i, j: (i, 0)),
            pl.BlockSpec((kn, tn), lambda i, j: (0, j)),
            pl.BlockSpec((tm, tn), lambda i, j: (i, j + off)),
            pl.BlockSpec((tm, tn), lambda i, j: (i, j + off + N // tn)),
        ],
        out_specs=pl.BlockSpec((tm, tn), lambda i, j: (i, j)),
        out_shape=jax.ShapeDtypeStruct((M, N), bf16),
        compiler_params=_cparams(("parallel", "arbitrary")),
        name="branch_merge",
    )(hg, w_rnn, o_nsa, w_nsa, proj, proj)


def _rglru_body(xr_ref, gr_ref, cw_ref, cb_ref, wa_ref, ba_ref, wi_ref, bi_ref, sp_ref, o_ref,
                tail_ref, carry_ref, a_ref, u_ref, *, tc):
    c = pl.program_id(1)

    @pl.when(c == 0)
    def _():
        tail_ref[...] = jnp.zeros_like(tail_ref)
        carry_ref[...] = jnp.zeros_like(carry_ref)

    x = xr_ref[...].astype(f32)
    d = x.shape[1]
    row = lax.broadcasted_iota(jnp.int32, (tc, 1), 0)
    row8 = lax.broadcasted_iota(jnp.int32, (SUBLANES, 1), 0)
    tail = tail_ref[...]
    xc = cb_ref[...] + cw_ref[CONV_WIDTH - 1:CONV_WIDTH, :] * x
    for j in range(1, CONV_WIDTH):
        cur = pltpu.roll(x, j, 0)
        halo = pltpu.roll(tail, j, 0)
        head = jnp.where(row8 < j, halo, cur[0:SUBLANES])
        sh = jnp.concatenate([head, cur[SUBLANES:]], axis=0)
        xc = xc + cw_ref[CONV_WIDTH - 1 - j:CONV_WIDTH - j, :] * sh
    tail_ref[...] = x[tc - SUBLANES:]

    xcb = xc.astype(bf16)
    for s in range(d // RNN_SUPER):
        sl = slice(s * RNN_SUPER, (s + 1) * RNN_SUPER)
        za = jnp.dot(xcb[:, sl], wa_ref[s], preferred_element_type=f32) + ba_ref[:, sl]
        zi = jnp.dot(xcb[:, sl], wi_ref[s], preferred_element_type=f32) + bi_ref[:, sl]
        log_a = (-LRU_C) * _sigmoid(za) * sp_ref[:, sl]
        a = jnp.exp(log_a)
        u = jnp.sqrt(-jnp.tanh(log_a) * (a * a + 1.0)) * (_sigmoid(zi) * xc[:, sl])
        r8 = row % SUBLANES
        for k in (1, 2, 4):
            a_s = pltpu.roll(a, k, 0)
            u_s = pltpu.roll(u, k, 0)
            m = r8 >= k
            u = jnp.where(m, a * u_s + u, u)
            a = jnp.where(m, a * a_s, a)
        a_ref[:, sl] = a
        u_ref[:, sl] = u

    def group(g, h_prev):
        r0 = pl.multiple_of(g * SUBLANES, SUBLANES)
        h = a_ref[pl.ds(r0, SUBLANES), :] * h_prev + u_ref[pl.ds(r0, SUBLANES), :]
        u_ref[pl.ds(r0, SUBLANES), :] = h
        return jnp.broadcast_to(h[SUBLANES - 1:SUBLANES, :], h.shape)

    carry_ref[...] = lax.fori_loop(0, tc // SUBLANES, group, carry_ref[...])
    o_ref[...] = (u_ref[...] * _gelu(gr_ref[...].astype(f32))).astype(o_ref.dtype)


def _rglru(proj, B, S, d_rnn, conv_w, conv_b, wa, ba, wi, bi, sp, *, tc):
    nc = S // tc
    ns = d_rnn // RNN_SUPER
    row = lambda b, c: (b * nc + c, 0)
    const2 = lambda b, c: (0, 0)
    const3 = lambda b, c: (0, 0, 0)
    return pl.pallas_call(
        functools.partial(_rglru_body, tc=tc),
        grid=(B, nc),
        in_specs=[
            pl.BlockSpec((tc, d_rnn), row),
            pl.BlockSpec((tc, d_rnn), lambda b, c: (b * nc + c, 1)),
            pl.BlockSpec((CONV_WIDTH, d_rnn), const2),
            pl.BlockSpec((1, d_rnn), const2),
            pl.BlockSpec((ns, RNN_SUPER, RNN_SUPER), const3),
            pl.BlockSpec((1, d_rnn), const2),
            pl.BlockSpec((ns, RNN_SUPER, RNN_SUPER), const3),
            pl.BlockSpec((1, d_rnn), const2),
            pl.BlockSpec((1, d_rnn), const2),
        ],
        out_specs=pl.BlockSpec((tc, d_rnn), row),
        out_shape=jax.ShapeDtypeStruct((B * S, d_rnn), bf16),
        scratch_shapes=[pltpu.VMEM((SUBLANES, d_rnn), f32), pltpu.VMEM((SUBLANES, d_rnn), f32),
                        pltpu.VMEM((tc, d_rnn), f32), pltpu.VMEM((tc, d_rnn), f32)],
        compiler_params=_cparams(("parallel", "arbitrary")),
        name="rglru",
    )(proj, proj, conv_w, conv_b, wa, ba, wi, bi, sp)


def _super_blocks(w):
    nb, bd, _ = w.shape
    per = RNN_SUPER // bd
    out = jnp.zeros((nb // per, RNN_SUPER, RNN_SUPER), w.dtype)
    for k in range(per):
        out = out.at[:, k * bd:(k + 1) * bd, k * bd:(k + 1) * bd].set(w[k::per])
    return out.astype(bf16)


def _compress_body(x_ref, pos_ref, w1_ref, w2_ref, gain_ref, o_ref):
    kind = pl.program_id(1)
    x = x_ref[...].astype(f32)
    nrow = x.shape[0]
    half = x.shape[1]
    first = (x + pos_ref[:, :half]).astype(bf16)
    second = (x + pos_ref[:, half:]).astype(bf16)
    p = jnp.dot(first, w1_ref[:half, :], preferred_element_type=f32)
    q = jnp.dot(second, w1_ref[half:, :], preferred_element_type=f32)
    hid = _gelu(p + pltpu.roll(q, nrow - 1, 0))
    out = jnp.dot(hid.astype(bf16), w2_ref[...], preferred_element_type=f32)
    out = jnp.where(kind == 0, _rms(out, gain_ref[...]), out)
    rowi = lax.broadcasted_iota(jnp.int32, out.shape, 0)
    o_ref[...] = jnp.where(rowi < nrow - 1, out, 0.0).astype(o_ref.dtype)


def _compress(xk, pos, w1, w2, gain):
    B, _, G, nrow, width = xk.shape
    hid = w1.shape[-1]
    return pl.pallas_call(
        _compress_body,
        grid=(B, 2, G),
        in_specs=[
            pl.BlockSpec((None, None, None, nrow, width), lambda b, k, g: (b, k, g, 0, 0)),
            pl.BlockSpec((None, 1, 2 * width), lambda b, k, g: (k, 0, 0)),
            pl.BlockSpec((None, 2 * width, hid), lambda b, k, g: (k, 0, 0)),
            pl.BlockSpec((None, hid, HEAD_DIM), lambda b, k, g: (k, 0, 0)),
            pl.BlockSpec((1, HEAD_DIM), lambda b, k, g: (0, 0)),
        ],
        out_specs=pl.BlockSpec((None, None, None, nrow, HEAD_DIM), lambda b, k, g: (b, k, g, 0, 0)),
        out_shape=jax.ShapeDtypeStruct((B, 2, G, nrow, HEAD_DIM), bf16),
        compiler_params=_cparams(("parallel", "parallel", "parallel")),
        name="compress",
    )(xk, pos, w1, w2, gain)


def _head_gate(gl, col):
    lane = lax.broadcasted_iota(jnp.int32, gl.shape, 1)
    return _sigmoid(jnp.sum(jnp.where(lane == col, gl, 0.0), axis=1, keepdims=True))


def _slope_table():
    sl = 2.0 ** (-8.0 * np.arange(1, N_HEADS + 1) / N_HEADS)
    tab = np.zeros((N_KV, 1, LANES), np.float32)
    tab[:, 0, :HEADS_PER_KV] = sl.reshape(N_KV, HEADS_PER_KV)
    return jnp.asarray(tab)


def _cmp_attn_body(q_ref, kc_ref, vc_ref, qg_ref, gl_ref, ov_ref, sl_ref, oc_ref, sel_ref, *, tq, n_cmp, n_sel):
    g = pl.program_id(1)
    t0 = pl.program_id(2) * tq
    q = q_ref[...].astype(f32)
    kc = kc_ref[...]
    vc = vc_ref[...]
    ncp = kc.shape[0]
    gl = gl_ref[...].astype(f32)
    t = t0 + lax.broadcasted_iota(jnp.int32, (tq, 1), 0)
    cidx = lax.broadcasted_iota(jnp.int32, (1, ncp), 1)
    cend = cidx * CMP_STRIDE + (CMP_BLOCK - 1)
    valid = (cend <= t) & (cidx < n_cmp)
    rel = (cend - t0).astype(f32)
    scale = HEAD_DIM ** -0.5
    imp = jnp.zeros((tq, LANES), f32)
    for h in range(HEADS_PER_KV):
        qh = (_rms(q[:, h * HEAD_DIM:(h + 1) * HEAD_DIM], qg_ref[...]) * scale).astype(bf16)
        s = lax.dot_general(qh, kc, (((1,), (1,)), ((), ())), preferred_element_type=f32)
        s = jnp.where(valid, s + sl_ref[:, h:h + 1] * rel, NEG)
        m = jnp.max(s, axis=1, keepdims=True)
        e = jnp.where(valid, jnp.exp(s - m), 0.0)
        den = jnp.sum(e, axis=1, keepdims=True)
        p = (e / jnp.where(den > 0.0, den, 1.0)).astype(bf16)
        o = jnp.dot(p, vc, preferred_element_type=f32)
        oc_ref[:, h * HEAD_DIM:(h + 1) * HEAD_DIM] = (o * _head_gate(gl, g * HEADS_PER_KV + h)).astype(oc_ref.dtype)
        imp = imp + jnp.dot(p, ov_ref[...], preferred_element_type=f32)

    j = lax.broadcasted_iota(jnp.int32, (1, LANES), 1)
    cur = t // SEL_BLOCK
    forced = (j == 0) | (j == cur) | (j == cur - 1)
    future = j * SEL_BLOCK > t
    imp = jnp.where(forced, FORCE, jnp.where(future, NEG, imp))
    imp = jnp.where(j < n_sel, imp, 2.0 * NEG)
    imp_t = imp.T[0:n_sel]
    row8 = lax.broadcasted_iota(jnp.int32, (SUBLANES, 1), 0)
    groups = [imp_t[r * SUBLANES:(r + 1) * SUBLANES] for r in range(n_sel // SUBLANES)]
    rows = []
    for jj in range(n_sel):
        v = imp_t[jj:jj + 1, :]
        part = jnp.zeros((SUBLANES, tq), f32)
        for r, blk in enumerate(groups):
            lo = r * SUBLANES
            if lo + SUBLANES <= jj:
                ahead = jnp.where(blk >= v, 1.0, 0.0)
            elif lo > jj:
                ahead = jnp.where(blk > v, 1.0, 0.0)
            else:
                ahead = jnp.where(row8 < jj - lo, jnp.where(blk >= v, 1.0, 0.0), jnp.where(blk > v, 1.0, 0.0))
            part = part + ahead
        rows.append(jnp.sum(part, axis=0, keepdims=True))
    rank = jnp.concatenate(rows, axis=0)
    neg_t = jnp.where(rank < float(min(SEL_TOPK, n_sel)), 0.0, NEG)
    pieces = [jnp.zeros((SEL_LANE0, tq), f32), neg_t]
    if SEL_LANE0 + n_sel < LANES:
        pieces.append(jnp.zeros((LANES - SEL_LANE0 - n_sel, tq), f32))
    sel_ref[...] = jnp.concatenate(pieces, axis=0).T.astype(sel_ref.dtype)


def _cmp_attn(proj, gl, kvc, q_gain, overlap, B, S, q_col0, *, tq):
    G = N_KV
    ncp = kvc.shape[3]
    n_cmp = (S - CMP_BLOCK) // CMP_STRIDE + 1
    n_sel = S // SEL_BLOCK
    nq = S // tq
    gw = HEADS_PER_KV * HEAD_DIM
    qb0 = q_col0 // gw
    assert SEL_LANE0 + n_sel <= LANES
    oc, sel = pl.pallas_call(
        functools.partial(_cmp_attn_body, tq=tq, n_cmp=n_cmp, n_sel=n_sel),
        grid=(B, G, nq),
        in_specs=[
            pl.BlockSpec((tq, gw), lambda b, g, i: (b * nq + i, qb0 + g)),
            pl.BlockSpec((None, None, None, ncp, HEAD_DIM), lambda b, g, i: (b, 0, g, 0, 0)),
            pl.BlockSpec((None, None, None, ncp, HEAD_DIM), lambda b, g, i: (b, 1, g, 0, 0)),
            pl.BlockSpec((1, HEAD_DIM), lambda b, g, i: (0, 0)),
            pl.BlockSpec((tq, LANES), lambda b, g, i: (b * nq + i, 0)),
            pl.BlockSpec((ncp, LANES), lambda b, g, i: (0, 0)),
            pl.BlockSpec((None, 1, LANES), lambda b, g, i: (g, 0, 0)),
        ],
        out_specs=[
            pl.BlockSpec((tq, gw), lambda b, g, i: (b * nq + i, g)),
            pl.BlockSpec((None, None, tq, LANES), lambda b, g, i: (b, g, i, 0)),
        ],
        out_shape=[jax.ShapeDtypeStruct((B * S, N_HEADS * HEAD_DIM), bf16),
                   jax.ShapeDtypeStruct((B, G, S, LANES), bf16)],
        compiler_params=_cparams(("parallel", "parallel", "parallel")),
        name="cmp_attn",
    )(proj, kvc, kvc, q_gain, gl, overlap, _slope_table())
    return oc, sel


def _sel_win_body(q_ref, ks_ref, vs_ref, kw_ref, vw_ref, qg_ref, kg_ref, gl_ref, sel_ref, oc_ref, sl_ref, o_ref,
                  ksa_ref, kwa_ref, s_ref, mx_ref, l_ref, acc_ref, *, tq):
    g = pl.program_id(1)
    qi = pl.program_id(2)
    hk = HEADS_PER_KV
    nw = WINDOW // tq
    nl = tq // LANES
    nt = (((1,), (1,)), ((), ()))

    @pl.when(qi == 0)
    def _():
        n_keys = ks_ref.shape[0]
        kpos = lax.broadcasted_iota(jnp.int32, (n_keys, LANES), 0)
        lane = lax.broadcasted_iota(jnp.int32, (n_keys, LANES), 1)
        blk = kpos // SEL_BLOCK
        pos_cols = jnp.where(lane < N_SLOPE_PARTS, blk.astype(f32),
                             jnp.where(lane < 2 * N_SLOPE_PARTS, (kpos % SEL_BLOCK).astype(f32), 0.0))
        onehot = jnp.where(lane - SEL_LANE0 == blk, 1.0, 0.0)
        ksa_ref[:, :HEAD_DIM] = _rms(ks_ref[...].astype(f32), kg_ref[0:1, :]).astype(bf16)
        ksa_ref[:, HEAD_DIM:] = (pos_cols + onehot).astype(bf16)
        kwa_ref[:, :HEAD_DIM] = _rms(kw_ref[...].astype(f32), kg_ref[1:2, :]).astype(bf16)
        kwa_ref[:, HEAD_DIM:] = pos_cols.astype(bf16)

    q = q_ref[...].astype(f32)
    scale = HEAD_DIM ** -0.5
    qx = sel_ref[...].astype(f32)
    qa = jnp.concatenate(
        [jnp.concatenate([(_rms(q[:, h * HEAD_DIM:(h + 1) * HEAD_DIM], qg_ref[...]) * scale).astype(bf16),
                          (qx + sl_ref[h:h + 1, :]).astype(bf16)], axis=1) for h in range(hk)],
        axis=0)
    r = lax.broadcasted_iota(jnp.int32, (tq, tq), 0)
    c = lax.broadcasted_iota(jnp.int32, (tq, tq), 1)
    mask_diag = jnp.concatenate([jnp.where(c <= r, 0.0, NEG)] * hk, axis=0)
    mask_edge = jnp.concatenate([jnp.where(c > r, 0.0, NEG)] * hk, axis=0)

    def reset():
        mx_ref[...] = jnp.full_like(mx_ref, NEG)
        l_ref[...] = jnp.zeros_like(l_ref)
        acc_ref[...] = jnp.zeros_like(acc_ref)

    def score_tile(ka_ref, kt, slot, mask):
        k0 = pl.multiple_of(kt * tq, tq)
        s = lax.dot_general(qa, ka_ref[pl.ds(k0, tq), :], nt, preferred_element_type=f32)
        if mask is not None:
            s = s + mask
        s_ref[slot] = s
        part = s[:, 0:LANES]
        for i in range(1, nl):
            part = jnp.maximum(part, s[:, i * LANES:(i + 1) * LANES])
        mx_ref[...] = jnp.maximum(mx_ref[...], part)

    def value_tile(v_ref, kt, slot, m):
        k0 = pl.multiple_of(kt * tq, tq)
        p = jnp.exp(s_ref[slot] - m)
        part = p[:, 0:LANES]
        for i in range(1, nl):
            part = part + p[:, i * LANES:(i + 1) * LANES]
        l_ref[...] += part
        acc_ref[...] += jnp.dot(p.astype(bf16), v_ref[pl.ds(k0, tq), :], preferred_element_type=f32)

    def finish():
        return acc_ref[...] / jnp.sum(l_ref[...], axis=1, keepdims=True)

    reset()

    def sel_scores(kt, carry):
        score_tile(ksa_ref, kt, kt, None)
        return carry

    lax.fori_loop(0, qi, sel_scores, 0)
    score_tile(ksa_ref, qi, qi, mask_diag)
    m_sel = jnp.max(mx_ref[...], axis=1, keepdims=True)

    def sel_values(kt, carry):
        value_tile(vs_ref, kt, kt, m_sel)
        return carry

    lax.fori_loop(0, qi + 1, sel_values, 0)
    o_sel = finish()

    reset()
    for d in range(nw, 0, -1):
        mask = mask_edge if d == nw else None
        pl.when(qi >= d)(functools.partial(score_tile, kwa_ref, qi - d, nw - d, mask))
    score_tile(kwa_ref, qi, nw, mask_diag)
    m_win = jnp.max(mx_ref[...], axis=1, keepdims=True)
    for d in range(nw, 0, -1):
        pl.when(qi >= d)(functools.partial(value_tile, vw_ref, qi - d, nw - d, m_win))
    value_tile(vw_ref, qi, nw, m_win)
    o_win = finish()

    gl = gl_ref[...].astype(f32)
    for h in range(hk):
        col = g * hk + h
        rows = slice(h * tq, (h + 1) * tq)
        o = (oc_ref[:, h * HEAD_DIM:(h + 1) * HEAD_DIM].astype(f32)
             + _head_gate(gl, N_HEADS + col) * o_sel[rows] + _head_gate(gl, 2 * N_HEADS + col) * o_win[rows])
        o_ref[:, h * HEAD_DIM:(h + 1) * HEAD_DIM] = o.astype(o_ref.dtype)


def _alibi_parts_table():
    sl = (2.0 ** (-8.0 * np.arange(1, N_HEADS + 1) / N_HEADS)).astype(np.float32)
    tab = np.zeros((N_HEADS, LANES), np.float32)
    rest = sl
    for i in range(N_SLOPE_PARTS):
        piece = rest.astype(bf16).astype(np.float32)
        rest = rest - piece
        tab[:, i] = piece * SEL_BLOCK
        tab[:, N_SLOPE_PARTS + i] = piece
    return jnp.asarray(tab.reshape(N_KV, HEADS_PER_KV, LANES))


def _sel_win_attn(proj, gl, sel, oc, q_gain, k_gains2, B, S, q_col0, kv_col0, *, tq):
    G = N_KV
    nq = S // tq
    gw = HEADS_PER_KV * HEAD_DIM
    qb0 = q_col0 // gw
    kb0 = kv_col0 // HEAD_DIM
    assert WINDOW % tq == 0 and tq % LANES == 0 and 2 * N_SLOPE_PARTS <= SEL_LANE0
    assert S // SEL_BLOCK <= 256
    n_slots = max(nq, WINDOW // tq + 1)

    def kv_spec(kind):
        return pl.BlockSpec((S, HEAD_DIM), lambda b, g, i, kind=kind: (b, kb0 + kind * G + g))

    return pl.pallas_call(
        functools.partial(_sel_win_body, tq=tq),
        grid=(B, G, nq),
        in_specs=[
            pl.BlockSpec((tq, gw), lambda b, g, i: (b * nq + i, qb0 + g)),
            kv_spec(2), kv_spec(3), kv_spec(4), kv_spec(5),
            pl.BlockSpec((1, HEAD_DIM), lambda b, g, i: (0, 0)),
            pl.BlockSpec((2, HEAD_DIM), lambda b, g, i: (0, 0)),
            pl.BlockSpec((tq, LANES), lambda b, g, i: (b * nq + i, 0)),
            pl.BlockSpec((None, None, tq, LANES), lambda b, g, i: (b, g, i, 0)),
            pl.BlockSpec((tq, gw), lambda b, g, i: (b * nq + i, g)),
            pl.BlockSpec((None, HEADS_PER_KV, LANES), lambda b, g, i: (g, 0, 0)),
        ],
        out_specs=pl.BlockSpec((tq, gw), lambda b, g, i: (b * nq + i, g)),
        out_shape=jax.ShapeDtypeStruct((B * S, N_HEADS * HEAD_DIM), bf16),
        scratch_shapes=[pltpu.VMEM((S, 2 * HEAD_DIM), bf16), pltpu.VMEM((S, 2 * HEAD_DIM), bf16),
                        pltpu.VMEM((n_slots, HEADS_PER_KV * tq, tq), f32),
                        pltpu.VMEM((HEADS_PER_KV * tq, LANES), f32), pltpu.VMEM((HEADS_PER_KV * tq, LANES), f32),
                        pltpu.VMEM((HEADS_PER_KV * tq, HEAD_DIM), f32)],
        compiler_params=_cparams(("parallel", "parallel", "arbitrary")),
        name="sel_win_attn",
    )(proj, proj, proj, proj, proj, q_gain, k_gains2, gl, sel, oc, _alibi_parts_table())


def _router_body(x_ref, gain_ref, wr_ref, br_ref, xn_ref, info_ref, cnt_ref, carry_ref, *, tm, n_exp):
    @pl.when(pl.program_id(0) == 0)
    def _():
        carry_ref[...] = jnp.zeros_like(carry_ref)

    xn = _rms(x_ref[...], gain_ref[...])
    xn_ref[...] = xn
    lane = lax.broadcasted_iota(jnp.int32, (tm, LANES), 1)
    x_hi = xn.astype(bf16)
    x_lo = (xn - x_hi.astype(f32)).astype(bf16)
    logits = (jnp.dot(x_hi, wr_ref[0], preferred_element_type=f32)
              + jnp.dot(x_lo, wr_ref[0], preferred_element_type=f32)
              + jnp.dot(x_hi, wr_ref[1], preferred_element_type=f32)) + br_ref[...]
    logits = jnp.where(lane < n_exp, logits, 2.0 * NEG)
    hots, vals = [], []
    for _ in range(TOP_K):
        mx = jnp.max(logits, axis=1, keepdims=True)
        idx = jnp.min(jnp.where(logits == mx, lane, LANES), axis=1, keepdims=True)
        hot = lane == idx
        hots.append(hot)
        vals.append(mx)
        logits = jnp.where(hot, 2.0 * NEG, logits)
    es = [jnp.exp(v - vals[0]) for v in vals]
    den = es[0] + es[1] + es[2] + es[3]
    onehot = jnp.zeros((tm, LANES), f32)
    for hot in hots:
        onehot = onehot + hot.astype(f32)
    r = lax.broadcasted_iota(jnp.int32, (tm, tm), 0)
    c = lax.broadcasted_iota(jnp.int32, (tm, tm), 1)
    before = jnp.dot((c < r).astype(bf16), onehot.astype(bf16), preferred_element_type=f32)
    rank = carry_ref[...] + before
    carry_ref[...] = carry_ref[...] + jnp.sum(onehot, axis=0, keepdims=True)
    cnt_ref[...] = carry_ref[...]
    info = jnp.zeros((tm, LANES), f32)
    for k in range(TOP_K):
        e_k = jnp.sum(jnp.where(hots[k], lane, 0), axis=1, keepdims=True).astype(f32)
        r_k = jnp.sum(jnp.where(hots[k], rank, 0.0), axis=1, keepdims=True)
        info = jnp.where(lane == k, e_k, info)
        info = jnp.where(lane == TOP_K + k, es[k] / den, info)
        info = jnp.where(lane == 2 * TOP_K + k, r_k, info)
    info_ref[...] = info


def _router(x, gain, wr, br, *, tm):
    T, D = x.shape
    return pl.pallas_call(
        functools.partial(_router_body, tm=tm, n_exp=N_EXPERTS),
        grid=(T // tm,),
        in_specs=[
            pl.BlockSpec((tm, D), lambda i: (i, 0)),
            pl.BlockSpec((1, D), lambda i: (0, 0)),
            pl.BlockSpec((2, D, LANES), lambda i: (0, 0, 0)),
            pl.BlockSpec((1, LANES), lambda i: (0, 0)),
        ],
        out_specs=[
            pl.BlockSpec((tm, D), lambda i: (i, 0)),
            pl.BlockSpec((tm, LANES), lambda i: (i, 0)),
            pl.BlockSpec((1, LANES), lambda i: (0, 0)),
        ],
        out_shape=[jax.ShapeDtypeStruct((T, D), f32), jax.ShapeDtypeStruct((T, LANES), f32),
                   jax.ShapeDtypeStruct((1, LANES), f32)],
        scratch_shapes=[pltpu.VMEM((1, LANES), f32)],
        compiler_params=_cparams(("arbitrary",)),
        name="router",
    )(x, gain, wr, br)


def _dispatch_body(pos_ref, xn_ref, xs_in_ref, xs_ref, sem, *, tm):
    del xs_in_ref

    def row_copy(t, k):
        return pltpu.make_async_copy(xn_ref.at[pl.ds(t, 1)], xs_ref.at[pl.ds(pos_ref[t * TOP_K + k], 1)], sem)

    def issue(t, c):
        for k in range(TOP_K):
            row_copy(t, k).start()
        return c

    def drain(t, c):
        for k in range(TOP_K):
            row_copy(t, k).wait()
        return c

    lax.fori_loop(0, tm, issue, 0)
    lax.fori_loop(0, tm, drain, 0)


def _dispatch(pos_flat, xn, n_rows, *, tm):
    T, D = xn.shape
    xs0 = jnp.zeros((n_rows, D), xn.dtype)
    return pl.pallas_call(
        functools.partial(_dispatch_body, tm=tm),
        grid=(T // tm,),
        in_specs=[
            pl.BlockSpec((tm * TOP_K,), lambda i: (i,), memory_space=pltpu.SMEM),
            pl.BlockSpec((tm, D), lambda i: (i, 0)),
            pl.BlockSpec(memory_space=pl.ANY),
        ],
        out_specs=pl.BlockSpec(memory_space=pl.ANY),
        out_shape=jax.ShapeDtypeStruct((n_rows, D), xn.dtype),
        scratch_shapes=[pltpu.SemaphoreType.DMA],
        input_output_aliases={2: 0},
        compiler_params=_cparams(("arbitrary",), disable_bounds_checks=True, has_side_effects=True),
        name="moe_dispatch",
    )(pos_flat, xn, xs0)


def _new_expert(be_ref, i):
    return (i == 0) | (be_ref[i] != be_ref[jnp.maximum(i - 1, 0)])


def _up_body(be_ref, nu_ref, x_ref, wg_ref, wl_ref, bg_ref, bl_ref, h_ref, wgs_ref, wls_ref):
    i = pl.program_id(1)

    @pl.when(_new_expert(be_ref, i) & (i < nu_ref[0]))
    def _():
        wgs_ref[...] = wg_ref[...].astype(bf16)
        wls_ref[...] = wl_ref[...].astype(bf16)

    @pl.when(i < nu_ref[0])
    def _():
        x = x_ref[...].astype(bf16)
        glu = jnp.minimum(jnp.dot(x, wgs_ref[...], preferred_element_type=f32) + bg_ref[...], SWIGLU_LIMIT)
        lin = jnp.clip(jnp.dot(x, wls_ref[...], preferred_element_type=f32) + bl_ref[...], -SWIGLU_LIMIT,
                       SWIGLU_LIMIT)
        h_ref[...] = (glu * _sigmoid(SWIGLU_ALPHA * glu) * (lin + 1.0)).astype(h_ref.dtype)

    @pl.when(i >= nu_ref[0])
    def _():
        h_ref[...] = jnp.zeros_like(h_ref)


def _moe_up(block_e, n_used, xs, w_up, b_up, *, rb, tf):
    R, D = xs.shape
    d_ff = w_up.shape[2] // 2
    nj = d_ff // tf
    nb = R // rb
    rowmap = lambda j, i, be, nu: (jnp.minimum(i, nu[0] - 1), 0)
    return pl.pallas_call(
        _up_body,
        grid_spec=pltpu.PrefetchScalarGridSpec(
            num_scalar_prefetch=2,
            grid=(nj, nb),
            in_specs=[
                pl.BlockSpec((rb, D), rowmap),
                pl.BlockSpec((None, D, tf), lambda j, i, be, nu: (be[i], 0, j)),
                pl.BlockSpec((None, D, tf), lambda j, i, be, nu: (be[i], 0, nj + j)),
                pl.BlockSpec((None, 1, tf), lambda j, i, be, nu: (be[i], 0, j)),
                pl.BlockSpec((None, 1, tf), lambda j, i, be, nu: (be[i], 0, nj + j)),
            ],
            out_specs=pl.BlockSpec((rb, tf), lambda j, i, be, nu: (i, j)),
            scratch_shapes=[pltpu.VMEM((D, tf), bf16), pltpu.VMEM((D, tf), bf16)],
        ),
        out_shape=jax.ShapeDtypeStruct((R, d_ff), bf16),
        compiler_params=pltpu.CompilerParams(dimension_semantics=("arbitrary", "arbitrary"),
                                             vmem_limit_bytes=MOE_VMEM_LIMIT),
        name="moe_up",
    )(block_e, n_used, xs, w_up, w_up, b_up, b_up)


def _down_body(be_ref, nu_ref, h_ref, w_ref, b_ref, y_ref, ws_ref):
    i = pl.program_id(1)

    @pl.when(_new_expert(be_ref, i) & (i < nu_ref[0]))
    def _():
        ws_ref[...] = w_ref[...].astype(bf16)

    @pl.when(i < nu_ref[0])
    def _():
        y_ref[...] = jnp.dot(h_ref[...], ws_ref[...], preferred_element_type=f32) + b_ref[...]

    @pl.when(i >= nu_ref[0])
    def _():
        y_ref[...] = jnp.zeros_like(y_ref)


def _moe_down(block_e, n_used, h, w_down, b_down, *, rb, tn):
    R, F = h.shape
    D = w_down.shape[2]
    return pl.pallas_call(
        _down_body,
        grid_spec=pltpu.PrefetchScalarGridSpec(
            num_scalar_prefetch=2,
            grid=(D // tn, R // rb),
            in_specs=[
                pl.BlockSpec((rb, F), lambda j, i, be, nu: (jnp.minimum(i, nu[0] - 1), 0)),
                pl.BlockSpec((None, F, tn), lambda j, i, be, nu: (be[i], 0, j)),
                pl.BlockSpec((None, 1, tn), lambda j, i, be, nu: (be[i], 0, j)),
            ],
            out_specs=pl.BlockSpec((rb, tn), lambda j, i, be, nu: (i, j)),
            scratch_shapes=[pltpu.VMEM((F, tn), bf16)],
        ),
        out_shape=jax.ShapeDtypeStruct((R, D), f32),
        compiler_params=pltpu.CompilerParams(dimension_semantics=("arbitrary", "arbitrary"),
                                             vmem_limit_bytes=MOE_VMEM_LIMIT),
        name="moe_down",
    )(block_e, n_used, h, w_down, b_down)


def _combine_body(pos_ref, info_ref, x_ref, ys_ref, o_ref, buf_ref, sem, *, tm):
    def row_copy(t, k):
        return pltpu.make_async_copy(ys_ref.at[pl.ds(pos_ref[t * TOP_K + k], 1)], buf_ref.at[k, pl.ds(t, 1)], sem)

    def issue(t, c):
        for k in range(TOP_K):
            row_copy(t, k).start()
        return c

    def drain(t, c):
        for k in range(TOP_K):
            row_copy(t, k).wait()
        return c

    lax.fori_loop(0, tm, issue, 0)
    lax.fori_loop(0, tm, drain, 0)
    info = info_ref[...]
    lane = lax.broadcasted_iota(jnp.int32, info.shape, 1)
    y = x_ref[...]
    for k in range(TOP_K):
        gate = jnp.sum(jnp.where(lane == TOP_K + k, info, 0.0), axis=1, keepdims=True)
        y = y + gate * buf_ref[k]
    o_ref[...] = y


def _combine(pos_flat, info, x, ys, *, tm):
    T, D = x.shape
    return pl.pallas_call(
        functools.partial(_combine_body, tm=tm),
        grid=(T // tm,),
        in_specs=[
            pl.BlockSpec((tm * TOP_K,), lambda i: (i,), memory_space=pltpu.SMEM),
            pl.BlockSpec((tm, LANES), lambda i: (i, 0)),
            pl.BlockSpec((tm, D), lambda i: (i, 0)),
            pl.BlockSpec(memory_space=pl.ANY),
        ],
        out_specs=pl.BlockSpec((tm, D), lambda i: (i, 0)),
        out_shape=jax.ShapeDtypeStruct((T, D), f32),
        scratch_shapes=[pltpu.VMEM((TOP_K, tm, D), f32), pltpu.SemaphoreType.DMA],
        compiler_params=_cparams(("arbitrary",), disable_bounds_checks=True),
        name="moe_combine",
    )(pos_flat, info, x, ys)


def _pick(n, *cands):
    for c in cands:
        if n % c == 0:
            return c
    return n


def _layer(x2d, p2d, B, S, norm_mix, w_in, conv_w, conv_b, w_rg_a, b_rg_a, w_rg_i, b_rg_i, lru_lambda, w_rnn_out,
           q_gain, k_gains, cmp_pos_k, cmp_w1_k, cmp_w2_k, cmp_pos_v, cmp_w1_v, cmp_w2_v, w_nsa_out, w_out,
           norm_moe, w_router, b_router, w_up, b_up, w_down, b_down, w_ple_proj, ple_norm, w_ple_gate):
    T, D = x2d.shape
    d_rnn = conv_w.shape[1]
    q_w = N_HEADS * HEAD_DIM
    kv_w = 6 * N_KV * HEAD_DIM
    n_gate = 3 * N_HEADS
    q0 = 2 * d_rnn
    kv0 = q0 + q_w
    gn0 = kv0 + kv_w
    gm0 = gn0 + n_gate
    tm = _pick(T, 1024, 512, 256)

    w_main = jnp.concatenate([w_in[:, :gn0], w_in[:, gm0:]], axis=1).astype(bf16)
    w_gate = jnp.pad(w_in[:, gn0:gm0], ((0, 0), (0, LANES - n_gate))).astype(bf16)
    proj = _matmul(x2d, w_main, tm=tm, tn=1024, out_dtype=bf16, gain=norm_mix, name="in_proj")
    gl = _matmul(x2d, w_gate, tm=tm, tn=LANES, out_dtype=f32, gain=norm_mix, name="in_proj_gates")
    merge0 = gn0

    sp = jax.nn.softplus(-lru_lambda.astype(f32)).reshape(1, d_rnn)
    hg = _rglru(proj, B, S, d_rnn, conv_w, conv_b.reshape(1, d_rnn), _super_blocks(w_rg_a),
                b_rg_a.reshape(1, d_rnn), _super_blocks(w_rg_i), b_rg_i.reshape(1, d_rnn), sp,
                tc=_pick(S, 256, 128))

    nrow = S // CMP_STRIDE
    kv_cmp = proj[:, kv0:kv0 + 2 * N_KV * HEAD_DIM].reshape(B, nrow, CMP_STRIDE, 2, N_KV, HEAD_DIM)
    kv_cmp = kv_cmp.transpose(0, 3, 4, 1, 2, 5).reshape(B, 2, N_KV, nrow, CMP_STRIDE * HEAD_DIM)
    pos = jnp.stack([cmp_pos_k, cmp_pos_v]).reshape(2, 1, CMP_BLOCK * HEAD_DIM).astype(f32)
    w1 = jnp.stack([cmp_w1_k, cmp_w1_v]).astype(bf16)
    w2 = jnp.stack([cmp_w2_k, cmp_w2_v]).astype(bf16)
    kvc = _compress(kv_cmp, pos, w1, w2, k_gains[0:1].astype(f32))

    n_cmp = (S - CMP_BLOCK) // CMP_STRIDE + 1
    n_sel = S // SEL_BLOCK
    c0 = np.arange(nrow)[:, None] * CMP_STRIDE
    s0 = np.arange(LANES)[None, :] * SEL_BLOCK
    overlap = np.clip(np.minimum(c0 + CMP_BLOCK, s0 + SEL_BLOCK) - np.maximum(c0, s0), 0, None) / CMP_STRIDE
    overlap = overlap * (np.arange(nrow)[:, None] < n_cmp) * (np.arange(LANES)[None, :] < n_sel)
    qg = q_gain.reshape(1, HEAD_DIM).astype(f32)
    oc, sel = _cmp_attn(proj, gl, kvc, qg, jnp.asarray(overlap, bf16), B, S, q0, tq=_pick(S, 256, 128))
    o_nsa = _sel_win_attn(proj, gl, sel, oc, qg, k_gains[1:3].astype(f32), B, S, q0, kv0,
                          tq=_pick(S, 256, 128))

    merged = _branch_merge(hg, w_rnn_out.astype(bf16), o_nsa, w_nsa_out.astype(bf16), proj, merge0,
                           tm=_pick(T, 512, 256), tn=1024)
    x1 = _matmul(merged, w_out.astype(bf16), tm=tm, tn=1024, out_dtype=f32, extras=[(x2d, 0)],
                 epilogue=lambda acc, xr: xr + acc, name="mix_out")

    x2 = _moe(x1, norm_moe, w_router, b_router, w_up, b_up, w_down, b_down)

    ple_gain = ple_norm.reshape(1, D).astype(f32)
    ple = _ple_proj(p2d, w_ple_proj.astype(bf16), ple_gain, tm=_pick(T, 512, 256))
    return _matmul(x2, w_ple_gate.astype(bf16), tm=tm, tn=1024, out_dtype=f32, extras=[(x2, 0), (ple, 0)],
                   epilogue=lambda acc, xr, pe: xr + _sigmoid(acc) * pe.astype(f32), name="ple_gate")


def _moe(x1, norm_moe, w_router, b_router, w_up, b_up, w_down, b_down):
    T, D = x1.shape
    wr_f = jnp.pad(w_router.astype(f32), ((0, 0), (0, LANES - N_EXPERTS)))
    wr_hi = wr_f.astype(bf16)
    wr = jnp.stack([wr_hi, (wr_f - wr_hi.astype(f32)).astype(bf16)])
    br = jnp.pad(b_router, (0, LANES - N_EXPERTS)).reshape(1, LANES).astype(f32)
    xn, info, cnt = _router(x1, norm_moe.reshape(1, D).astype(f32), wr, br, tm=_pick(T, 512, 256))
    rb = MOE_ROWS
    eid = info[:, 0:TOP_K].astype(jnp.int32)
    rank = info[:, 2 * TOP_K:3 * TOP_K].astype(jnp.int32)
    counts = cnt[0, :N_EXPERTS].astype(jnp.int32)
    padded = (counts + rb - 1) // rb * rb
    pend = jnp.cumsum(padded)
    pstart = pend - padded
    pos_flat = (pstart[eid] + rank).reshape(T * TOP_K)
    n_blocks = (T * TOP_K) // rb + N_EXPERTS
    block_e = jnp.minimum(jnp.sum(jnp.arange(n_blocks)[:, None] * rb >= pend[None, :], axis=1),
                          N_EXPERTS - 1).astype(jnp.int32)
    n_used = (pend[-1:] // rb).astype(jnp.int32)
    xs = _dispatch(pos_flat, xn, n_blocks * rb, tm=_pick(T, 256))
    hmid = _moe_up(block_e, n_used, xs, w_up, b_up.reshape(N_EXPERTS, 1, -1).astype(f32), rb=rb, tf=1024)
    ys = _moe_down(block_e, n_used, hmid, w_down, b_down.reshape(N_EXPERTS, 1, -1).astype(f32), rb=rb,
                   tn=w_down.shape[2])
    return _combine(pos_flat, info, x1, ys, tm=_pick(T, 256))


def _ple_body(p_ref, w_ref, g_ref, o_ref):
    acc = jnp.dot(p_ref[...].astype(bf16), w_ref[...], preferred_element_type=f32)
    o_ref[...] = _rms(acc, g_ref[...]).astype(o_ref.dtype)


def _ple_proj(p2d, w, gain, *, tm):
    T, K = p2d.shape
    D = w.shape[1]
    return pl.pallas_call(
        _ple_body,
        grid=(T // tm,),
        in_specs=[pl.BlockSpec((tm, K), lambda i: (i, 0)), pl.BlockSpec((K, D), lambda i: (0, 0)),
                  pl.BlockSpec((1, D), lambda i: (0, 0))],
        out_specs=pl.BlockSpec((tm, D), lambda i: (i, 0)),
        out_shape=jax.ShapeDtypeStruct((T, D), bf16),
        compiler_params=_cparams(("parallel",)),
        name="ple_proj",
    )(p2d, w, gain)


def kernel(x, p, norm_mix, w_in, conv_w, conv_b, w_rg_a, b_rg_a, w_rg_i, b_rg_i, lru_lambda, w_rnn_out, q_gain, k_gains, cmp_pos_k, cmp_w1_k, cmp_w2_k, cmp_pos_v, cmp_w1_v, cmp_w2_v, w_nsa_out, w_out, norm_moe, w_router, b_router, w_up, b_up, w_down, b_down, w_ple_proj, ple_norm, w_ple_gate):
    B, S, D = x.shape
    depth = p.shape[0]
    x2d = x.reshape(B * S, D)
    for i in range(depth):
        x2d = _layer(x2d, p[i].reshape(B * S, -1), B, S, norm_mix[i], w_in[i], conv_w[i], conv_b[i], w_rg_a[i],
                     b_rg_a[i], w_rg_i[i], b_rg_i[i], lru_lambda[i], w_rnn_out[i], q_gain[i], k_gains[i],
                     cmp_pos_k[i], cmp_w1_k[i], cmp_w2_k[i], cmp_pos_v[i], cmp_w1_v[i], cmp_w2_v[i], w_nsa_out[i],
                     w_out[i], norm_moe[i], w_router[i], b_router[i], w_up[i], b_up[i], w_down[i], b_down[i],
                     w_ple_proj[i], ple_norm[i], w_ple_gate[i])
    return x2d.reshape(B, S, D)
```
